```python
import jax, jax.numpy as jnp
from jax import lax
import numpy as np

D_MODEL = 2048
BATCH = 1
SEQ = 8192
DEPTH = 4
DEC_BATCH = 32
DEC_SEQ = 64
PAST_LEN = 2048

CHUNK = 64
Q_BLOCK = 128
N_MIXERS = 4
GROUP_W = D_MODEL // N_MIXERS
HEAD_DIM = 128
N_HEADS = GROUP_W // HEAD_DIM
D_MIX = N_MIXERS * GROUP_W
D_IN = 12 * GROUP_W
POOL_WINDOWS = (2, 4, 8, 16)
N_POOL = len(POOL_WINDOWS)
POOL_CH = GROUP_W // N_POOL
POOL_PAD = max(POOL_WINDOWS) - 1
D_FF = 4 * D_MODEL
ROPE_BASE = 10000.0
EPS = 1e-6

kernel_name = 'hybrid_stream_ret_sb_pool_hgrn2_step'


def rms_norm(x, g):
    xf = x.astype(jnp.float32)
    y = xf * lax.rsqrt(jnp.mean(xf * xf, axis=-1, keepdims=True) + EPS)
    return (y * g.astype(jnp.float32)).astype(x.dtype)


def rope(x, pos):
    half = HEAD_DIM // 2
    inv = ROPE_BASE ** (-jnp.arange(half, dtype=jnp.float32) / half)
    ang = pos.astype(jnp.float32)[:, None] * inv[None, :]
    cos = jnp.cos(ang)[None, :, None, :]
    sin = jnp.sin(ang)[None, :, None, :]
    x1, x2 = x[..., :half], x[..., half:]
    return jnp.concatenate([x1 * cos - x2 * sin, x1 * sin + x2 * cos], axis=-1).astype(x.dtype)


def split_heads(a):
    return a.reshape(a.shape[0], a.shape[1], N_HEADS, HEAD_DIM)


def to_chunks(a, L):
    B, T, H, D = a.shape
    return a.reshape(B, T // L, L, H, D).transpose(1, 0, 3, 2, 4)


def from_chunks(a):
    n, B, H, L, D = a.shape
    return a.transpose(1, 0, 3, 2, 4).reshape(B, n * L, H, D)


def retention(q, k, v, state):
    f32 = jnp.float32
    T = q.shape[1]
    L = min(T, CHUNK)
    log_gamma = jnp.log(1.0 - 2.0 ** (-5.0 - jnp.arange(N_HEADS, dtype=f32)))
    idx = jnp.arange(L, dtype=f32)
    diff = idx[:, None] - idx[None, :]
    intra = jnp.where(diff >= 0, jnp.exp(jnp.maximum(diff, 0.0)[None] * log_gamma[:, None, None]), 0.0)
    q_dec = jnp.exp((idx + 1.0)[None, :] * log_gamma[:, None])[None, :, :, None]
    k_dec = jnp.exp((L - 1.0 - idx)[None, :] * log_gamma[:, None])[None, :, :, None]
    c_dec = jnp.exp(L * log_gamma)[None, :, None, None]
    qc, kc, vc = (to_chunks(a.astype(f32), L) for a in (q, k, v))

    def step(S, blk):
        qi, ki, vi = blk
        scores = jnp.einsum('bhld,bhmd->bhlm', qi, ki) * intra
        o = jnp.einsum('bhlm,bhme->bhle', scores, vi) + jnp.einsum('bhld,bhde->bhle', qi * q_dec, S)
        S = c_dec * S + jnp.einsum('bhld,bhle->bhde', ki * k_dec, vi)
        return S, o

    S, o = lax.scan(step, state.astype(f32), (qc, kc, vc))
    return from_chunks(o), S


def hgrn2(q, f_logit, i, lb, state):
    f32 = jnp.float32
    T = q.shape[1]
    L = min(T, CHUNK)
    fl = f_logit.astype(f32)
    lbh = lb.reshape(N_HEADS, HEAD_DIM)
    log_f = jnp.logaddexp(jnp.log(lbh), jnp.log1p(-lbh) + jax.nn.log_sigmoid(fl))
    k = (1.0 - lbh) * jax.nn.sigmoid(-fl)
    causal = jnp.tril(jnp.ones((L, L), dtype=bool))[None, None, :, :, None]
    qc, kc, vc, gc = (to_chunks(a, L) for a in (q.astype(f32), k, i.astype(f32), log_f))

    def step(S, blk):
        qi, ki, vi, gi = blk
        G = jnp.cumsum(gi, axis=2)
        rel = jnp.where(causal, G[:, :, :, None, :] - G[:, :, None, :, :], -jnp.inf)
        scores = jnp.einsum('bhld,bhmd,bhlmd->bhlm', qi, ki, jnp.exp(rel))
        o = jnp.einsum('bhlm,bhme->bhle', scores, vi) + jnp.einsum('bhld,bhde->bhle', qi * jnp.exp(G), S)
        G_last = G[:, :, -1:, :]
        S = jnp.exp(G_last[:, :, 0, :])[..., None] * S + jnp.einsum('bhld,bhle->bhde', ki * jnp.exp(G_last - G), vi)
        return S, o

    S, o = lax.scan(step, state.astype(f32), (qc, kc, vc, gc))
    return from_chunks(o), S


def sb_block(q, k, v, q_pos, k_pos):
    z = jnp.einsum('bqhd,bkhd->bhqk', q, k) * (HEAD_DIM ** -0.5)
    valid = (k_pos[None, :] < q_pos[:, None])[None, None]
    log_fail = jnp.where(valid, jax.nn.log_sigmoid(-z), 0.0)
    later = lax.cumsum(log_fail, axis=3, reverse=True) - log_fail
    A = jnp.where(valid, jnp.exp(jax.nn.log_sigmoid(z) + later), 0.0)
    return jnp.einsum('bhqk,bkhd->bqhd', A, v)


def stick_breaking(q, k, v, q_pos, k_pos):
    f32 = jnp.float32
    q, k, v = q.astype(f32), k.astype(f32), v.astype(f32)
    B, T, H, D = q.shape
    if T <= Q_BLOCK:
        return sb_block(q, k, v, q_pos, k_pos)
    nb = T // Q_BLOCK
    qb = q.reshape(B, nb, Q_BLOCK, H, D).transpose(1, 0, 2, 3, 4)
    pb = q_pos.reshape(nb, Q_BLOCK)
    o = lax.map(lambda blk: sb_block(blk[0], k, v, blk[1], k_pos), (qb, pb))
    return o.transpose(1, 0, 2, 3, 4).reshape(B, T, H, D)


def multi_pool(u_ext, pos, pool_w, pool_scale):
    B, Te, C = u_ext.shape
    T = Te - POOL_PAD
    uf = u_ext.astype(jnp.float32)
    cs = jnp.concatenate([jnp.zeros((B, 1, C), jnp.float32), jnp.cumsum(uf, axis=1)], axis=1)
    end = cs[:, POOL_PAD + 1:POOL_PAD + 1 + T]
    tok = uf[:, POOL_PAD:]
    outs = []
    for gi, w in enumerate(POOL_WINDOWS):
        sl = slice(gi * POOL_CH, (gi + 1) * POOL_CH)
        start = cs[:, POOL_PAD + 1 - w:POOL_PAD + 1 - w + T, sl]
        cnt = jnp.minimum(pos + 1, w).astype(jnp.float32)[None, :, None]
        outs.append((end[..., sl] - start) / cnt - tok[..., sl])
    p = jnp.stack(outs, axis=2)
    y = jnp.einsum('btgc,gcd->btgd', p, pool_w.astype(jnp.float32)) * pool_scale.reshape(N_POOL, POOL_CH).astype(jnp.float32)
    return y.reshape(B, T, C).astype(u_ext.dtype)


def token_mixers(h, pos, ret_s, sb_k_past, sb_v_past, pool_s, hg_s, w_in, ret_g, pool_w, pool_scale, lb, hg_g):
    B, T, _ = h.shape
    dt = h.dtype
    z = h @ w_in
    r_q, r_k, r_v, r_g, s_q, s_k, s_v, p_u, g_q, g_f, g_i, g_g = jnp.split(z, 12, axis=-1)
    rq = rope(split_heads(r_q), pos)
    rk = rope(split_heads(r_k), pos) * (HEAD_DIM ** -0.5)
    ro, ret_new = retention(rq, rk, split_heads(r_v), ret_s)
    ro = rms_norm(ro.astype(dt), ret_g.reshape(N_HEADS, HEAD_DIM)).reshape(B, T, GROUP_W) * jax.nn.silu(r_g)
    sk, sv = split_heads(s_k), split_heads(s_v)
    k_all = jnp.concatenate([sb_k_past, sk], axis=1)
    v_all = jnp.concatenate([sb_v_past, sv], axis=1)
    k_pos = jnp.arange(k_all.shape[1], dtype=jnp.int32)
    so = stick_breaking(split_heads(s_q), k_all, v_all, pos, k_pos).astype(dt).reshape(B, T, GROUP_W)
    u_ext = jnp.concatenate([pool_s, p_u], axis=1)
    po = multi_pool(u_ext, pos, pool_w, pool_scale)
    pool_new = u_ext[:, -POOL_PAD:]
    ho, hg_new = hgrn2(jax.nn.silu(split_heads(g_q)), split_heads(g_f), split_heads(g_i), lb, hg_s)
    ho = rms_norm(ho.astype(dt), hg_g.reshape(N_HEADS, HEAD_DIM)).reshape(B, T, GROUP_W) * jax.nn.sigmoid(g_g)
    mix = jnp.concatenate([ro, so, po, ho], axis=-1)
    return mix, (ret_new.astype(dt), sk, sv, pool_new, hg_new.astype(dt))


def run_trunk(x, pos, ret_s, sb_k, sb_v, pool_s, hg_s, norm1_g, w_in, ret_norm_g, pool_w, pool_scale,
              lb_all, hg_norm_g, w_out, norm2_g, w_up, w_down, final_norm_g):
    rets, ks, vs, pools, hgs = [], [], [], [], []
    for l in range(DEPTH):
        h = rms_norm(x, norm1_g[l])
        mix, st = token_mixers(h, pos, ret_s[l], sb_k[l], sb_v[l], pool_s[l], hg_s[l], w_in[l],
                               ret_norm_g[l], pool_w[l], pool_scale[l], lb_all[l], hg_norm_g[l])
        x = x + mix @ w_out[l]
        h = rms_norm(x, norm2_g[l])
        x = x + jnp.square(jax.nn.relu(h @ w_up[l])) @ w_down[l]
        rets.append(st[0]); ks.append(st[1]); vs.append(st[2]); pools.append(st[3]); hgs.append(st[4])
    y = rms_norm(x, final_norm_g)
    return y, jnp.stack(rets), jnp.stack(ks), jnp.stack(vs), jnp.stack(pools), jnp.stack(hgs)


def setup_inputs(seed: int = 0) -> dict:
    key = jax.random.key(seed)
    ks = jax.random.split(key, 20)
    f32 = jnp.float32

    def nrm(k, shape, s):
        return jax.random.normal(k, shape, f32) * s

    return {
        'x_prompt': nrm(ks[0], (BATCH, SEQ, D_MODEL), 1.0),
        'x_sample': nrm(ks[1], (DEC_BATCH, DEC_SEQ, D_MODEL), 1.0),
        'state_ret': nrm(ks[2], (DEPTH, DEC_BATCH, N_HEADS, HEAD_DIM, HEAD_DIM), 0.3),
        'cache_sb_k': nrm(ks[3], (DEPTH, DEC_BATCH, PAST_LEN, N_HEADS, HEAD_DIM), 1.0),
        'cache_sb_v': nrm(ks[4], (DEPTH, DEC_BATCH, PAST_LEN, N_HEADS, HEAD_DIM), 1.0),
        'state_pool': nrm(ks[5], (DEPTH, DEC_BATCH, POOL_PAD, GROUP_W), 1.0),
        'state_hgrn': nrm(ks[6], (DEPTH, DEC_BATCH, N_HEADS, HEAD_DIM, HEAD_DIM), 0.5),
        'norm1_g': 1.0 + nrm(ks[7], (DEPTH, D_MODEL), 0.02),
        'w_in': nrm(ks[8], (DEPTH, D_MODEL, D_IN), D_MODEL ** -0.5),
        'ret_norm_g': 1.0 + nrm(ks[9], (DEPTH, GROUP_W), 0.02),
        'pool_w': nrm(ks[10], (DEPTH, N_POOL, POOL_CH, POOL_CH), POOL_CH ** -0.5),
        'pool_scale': 1.0 + nrm(ks[11], (DEPTH, GROUP_W), 0.02),
        'hg_lower_bounds': nrm(ks[12], (DEPTH, GROUP_W), 0.1),
        'hg_norm_g': 1.0 + nrm(ks[13], (DEPTH, GROUP_W), 0.02),
        'w_out': nrm(ks[14], (DEPTH, D_MIX, D_MODEL), D_MIX ** -0.5),
        'norm2_g': 1.0 + nrm(ks[15], (DEPTH, D_MODEL), 0.02),
        'w_up': nrm(ks[16], (DEPTH, D_MODEL, D_FF), D_MODEL ** -0.5),
        'w_down': nrm(ks[17], (DEPTH, D_FF, D_MODEL), D_FF ** -0.5),
        'final_norm_g': 1.0 + nrm(ks[18], (D_MODEL,), 0.02),
    }


def reference(x_prompt, x_sample, state_ret, cache_sb_k, cache_sb_v, state_pool, state_hgrn,
              norm1_g, w_in, ret_norm_g, pool_w, pool_scale, hg_lower_bounds, hg_norm_g,
              w_out, norm2_g, w_up, w_down, final_norm_g):
    lb_all = jnp.cumsum(jax.nn.softmax(hg_lower_bounds.astype(jnp.float32), axis=0), axis=0)
    lb_all = lb_all - lb_all[0:1]
    dt = x_prompt.dtype
    b_p, t_p = x_prompt.shape[0], x_prompt.shape[1]
    t_s = x_sample.shape[1]
    past = cache_sb_k.shape[2]
    zero_state = jnp.zeros((DEPTH, b_p, N_HEADS, HEAD_DIM, HEAD_DIM), dt)
    empty_kv = jnp.zeros((DEPTH, b_p, 0, N_HEADS, HEAD_DIM), dt)
    zero_pool = jnp.zeros((DEPTH, b_p, POOL_PAD, GROUP_W), dt)
    pos_p = jnp.arange(t_p, dtype=jnp.int32)
    pos_s = past + jnp.arange(t_s, dtype=jnp.int32)
    y_prompt, ret_prompt, sbk_prompt, sbv_prompt, pool_prompt, hgrn_prompt = run_trunk(
        x_prompt, pos_p, zero_state, empty_kv, empty_kv, zero_pool, zero_state,
        norm1_g, w_in, ret_norm_g, pool_w, pool_scale, lb_all, hg_norm_g, w_out, norm2_g, w_up, w_down, final_norm_g)
    y_sample, ret_sample, sbk_sample, sbv_sample, pool_sample, hgrn_sample = run_trunk(
        x_sample, pos_s, state_ret, cache_sb_k, cache_sb_v, state_pool, state_hgrn,
        norm1_g, w_in, ret_norm_g, pool_w, pool_scale, lb_all, hg_norm_g, w_out, norm2_g, w_up, w_down, final_norm_g)
    return (y_prompt, y_sample, ret_prompt, ret_sample, sbk_prompt, sbv_prompt, sbk_sample, sbv_sample,
            pool_prompt, pool_sample, hgrn_prompt, hgrn_sample)
```

```python
import functools
import math

import numpy as np
import jax
import jax.numpy as jnp
from jax import lax
from jax.experimental import pallas as pl
from jax.experimental.pallas import tpu as pltpu

F32 = jnp.float32
BF16 = jnp.bfloat16

HEAD_DIM = 128
N_HEADS = 4
GROUP_W = N_HEADS * HEAD_DIM
N_SLOTS = 12
POOL_WINDOWS = (2, 4, 8, 16)
POOL_PAD = max(POOL_WINDOWS) - 1
ROPE_BASE = 10000.0
EPS = 1e-6
SB_SCALE = HEAD_DIM ** -0.5
SB_DEAD_LOG = -104.0

VMEM_LIMIT = 48 * 1024 * 1024
ROW_GROUP = 256


def _cparams(*sem):
    return pltpu.CompilerParams(dimension_semantics=sem, vmem_limit_bytes=VMEM_LIMIT)


def _dot(a, b):
    return jnp.dot(a, b, preferred_element_type=F32)


def _dot_nt(a, b):
    return lax.dot_general(a, b, (((1,), (1,)), ((), ())), preferred_element_type=F32)


def _sigmoid(x):
    return 1.0 / (1.0 + jnp.exp(-x))


def _log_sigmoid(x):
    return jnp.minimum(x, 0.0) - jnp.log1p(jnp.exp(-jnp.abs(x)))


def _rms(x, g):
    ms = jnp.mean(x * x, axis=-1, keepdims=True)
    return x * lax.rsqrt(ms + EPS) * g


def _norm_matmul_kernel(x_ref, g_ref, w_ref, o_ref, h_ref):
    @pl.when(pl.program_id(1) == 0)
    def _():
        h_ref[...] = _rms(x_ref[...], g_ref[...]).astype(BF16)

    o_ref[...] = _dot(h_ref[...], w_ref[...])


def norm_matmul(x, g, w, layer, *, tm=512, tn=1024):
    t, d = x.shape
    n = w.shape[2]
    tm = min(tm, t)
    return pl.pallas_call(
        _norm_matmul_kernel,
        grid=(t // tm, n // tn),
        in_specs=[
            pl.BlockSpec((tm, d), lambda i, j: (i, 0)),
            pl.BlockSpec((1, d), lambda i, j: (0, 0)),
            pl.BlockSpec((None, d, tn), lambda i, j: (layer, 0, j)),
        ],
        out_specs=pl.BlockSpec((tm, tn), lambda i, j: (i, j)),
        out_shape=jax.ShapeDtypeStruct((t, n), F32),
        scratch_shapes=[pltpu.VMEM((tm, d), BF16)],
        compiler_params=_cparams("parallel", "arbitrary"),
        name="norm_in_proj",
    )(x, g.reshape(1, d), w)


def _out_proj_kernel(x_ref, m0_ref, m1_ref, m2_ref, m3_ref, w_ref, o_ref):
    acc = x_ref[...]
    for gi, m_ref in enumerate((m0_ref, m1_ref, m2_ref, m3_ref)):
        acc = acc + _dot(m_ref[...], w_ref[gi * GROUP_W:(gi + 1) * GROUP_W, :])
    o_ref[...] = acc


def out_proj(x, mixes, w, layer, *, tm=512):
    t, d = x.shape
    tm = min(tm, t)
    mix_spec = pl.BlockSpec((tm, GROUP_W), lambda i: (i, 0))
    return pl.pallas_call(
        _out_proj_kernel,
        grid=(t // tm,),
        in_specs=[pl.BlockSpec((tm, d), lambda i: (i, 0)), mix_spec, mix_spec, mix_spec, mix_spec,
                  pl.BlockSpec((None,) + w.shape[1:], lambda i: (layer, 0, 0))],
        out_specs=pl.BlockSpec((tm, d), lambda i: (i, 0)),
        out_shape=jax.ShapeDtypeStruct((t, d), F32),
        compiler_params=_cparams("parallel"),
        name="out_proj",
    )(x, *mixes, w)


def _mlp_kernel(x_ref, g_ref, wu_ref, wd_ref, o_ref, h_ref):
    @pl.when(pl.program_id(1) == 0)
    def _():
        x = x_ref[...]
        h_ref[...] = _rms(x, g_ref[...]).astype(BF16)
        o_ref[...] = x

    u = _dot(h_ref[...], wu_ref[...])
    a = jnp.square(jnp.maximum(u, 0.0)).astype(BF16)
    o_ref[...] += _dot(a, wd_ref[...])


def mlp(x, g, w_up, w_down, layer, *, tm=512, tf=1024):
    t, d = x.shape
    dff = w_up.shape[2]
    tm = min(tm, t)
    return pl.pallas_call(
        _mlp_kernel,
        grid=(t // tm, dff // tf),
        in_specs=[
            pl.BlockSpec((tm, d), lambda i, f: (i, 0)),
            pl.BlockSpec((1, d), lambda i, f: (0, 0)),
            pl.BlockSpec((None, d, tf), lambda i, f: (layer, 0, f)),
            pl.BlockSpec((None, tf, d), lambda i, f: (layer, f, 0)),
        ],
        out_specs=pl.BlockSpec((tm, d), lambda i, f: (i, 0)),
        out_shape=jax.ShapeDtypeStruct((t, d), F32),
        scratch_shapes=[pltpu.VMEM((tm, d), BF16)],
        compiler_params=_cparams("parallel", "arbitrary"),
        name="mlp",
    )(x, g.reshape(1, d), w_up, w_down)


def _final_norm_kernel(x_ref, g_ref, o_ref):
    o_ref[...] = _rms(x_ref[...], g_ref[...])


def final_norm(x, g, *, tm=512):
    t, d = x.shape
    tm = min(tm, t)
    return pl.pallas_call(
        _final_norm_kernel,
        grid=(t // tm,),
        in_specs=[pl.BlockSpec((tm, d), lambda i: (i, 0)), pl.BlockSpec((1, d), lambda i: (0, 0))],
        out_specs=pl.BlockSpec((tm, d), lambda i: (i, 0)),
        out_shape=jax.ShapeDtypeStruct((t, d), F32),
        compiler_params=_cparams("parallel"),
        name="final_norm",
    )(x, g.reshape(1, d))


def _zcol(slot):
    return lambda h, g: (g, slot * N_HEADS + h)


def _head_norm_gate(o, gn, gate):
    return (_rms(o, gn) * gate).astype(BF16)


def _retention_kernel(lg_ref, q_ref, k_ref, v_ref, g_ref, cos_ref, sin_ref, gn_ref, *rest,
                      chunk, nseq, carry):
    if carry:
        o_ref, sout_ref, dmask_ref, qdec_ref, kdec_ref, state_ref = rest
        s0_ref = None
    else:
        s0_ref, o_ref, sout_ref, dmask_ref, qdec_ref, kdec_ref = rest
        state_ref = None
    rows = chunk * nseq
    h = pl.program_id(0)
    g = pl.program_id(1)

    @pl.when(g == 0)
    def _():
        lg = lg_ref[h]
        row = lax.broadcasted_iota(jnp.int32, (rows, rows), 0)
        col = lax.broadcasted_iota(jnp.int32, (rows, rows), 1)
        diff = row - col
        ok = jnp.where((row ^ col) < chunk, diff, -1) >= 0
        dmask_ref[...] = jnp.where(ok, jnp.exp(jnp.maximum(diff, 0).astype(F32) * lg), 0.0)
        pos = (lax.broadcasted_iota(jnp.int32, (rows, HEAD_DIM), 0) & (chunk - 1)).astype(F32)
        qdec_ref[...] = jnp.exp((pos + 1.0) * lg)
        kdec_ref[...] = jnp.exp((chunk - 1.0 - pos) * lg)
        if carry:
            state_ref[...] = jnp.zeros_like(state_ref)

    cos = cos_ref[...]
    sin = sin_ref[...]

    def rope(x):
        return x * cos + pltpu.roll(x, HEAD_DIM // 2, 1) * sin

    q = rope(q_ref[...])
    k = rope(k_ref[...]) * SB_SCALE
    vb = v_ref[...].astype(BF16)
    scores = _dot_nt(q.astype(BF16), k.astype(BF16)) * dmask_ref[...]
    o_intra = _dot(scores.astype(BF16), vb)
    qd = (q * qdec_ref[...]).astype(BF16)
    kd = k * kdec_ref[...]
    cdec = qdec_ref[chunk - 1:chunk, :]
    outs = []
    for b in range(nseq):
        sl = slice(b * chunk, (b + 1) * chunk)
        s_old = state_ref[...] if carry else s0_ref[b]
        outs.append(o_intra[sl] + _dot(qd[sl], s_old.astype(BF16)))
        s_new = cdec * s_old + _dot(kd[sl].T.astype(BF16), vb[sl])
        if carry:
            state_ref[...] = s_new
        else:
            sout_ref[b] = s_new
    o = outs[0] if nseq == 1 else jnp.concatenate(outs, axis=0)
    gate = g_ref[...]
    o_ref[...] = _head_norm_gate(o, gn_ref[...], gate * _sigmoid(gate))

    if carry:
        @pl.when(g == pl.num_programs(1) - 1)
        def _():
            sout_ref[...] = state_ref[...]


def retention(z, cos, sin, gn, state, layer, *, batch, seq):
    t = z.shape[0]
    carry = state is None
    rows = min(ROW_GROUP, t)
    if carry:
        assert batch == 1 and t % rows == 0
        chunk, nseq = rows, 1
    else:
        chunk = seq
        nseq = rows // chunk
        assert rows % chunk == 0 and batch % nseq == 0
    ngroups = t // rows
    lg = jnp.asarray(np.log(1.0 - 2.0 ** (-5.0 - np.arange(N_HEADS))), F32)
    blk = (rows, HEAD_DIM)
    tab_idx = (lambda h, g: (g, 0)) if carry else (lambda h, g: (0, 0))
    in_specs = [
        pl.BlockSpec(memory_space=pltpu.SMEM),
        pl.BlockSpec(blk, _zcol(0)), pl.BlockSpec(blk, _zcol(1)),
        pl.BlockSpec(blk, _zcol(2)), pl.BlockSpec(blk, _zcol(3)),
        pl.BlockSpec(blk, tab_idx), pl.BlockSpec(blk, tab_idx),
        pl.BlockSpec((None, 1, HEAD_DIM), lambda h, g: (h, 0, 0)),
    ]
    args = [lg, z, z, z, z, cos, sin, gn.reshape(N_HEADS, 1, HEAD_DIM)]
    scratch = [pltpu.VMEM((rows, rows), F32), pltpu.VMEM(blk, F32), pltpu.VMEM(blk, F32)]
    if carry:
        sout_shape = jax.ShapeDtypeStruct((N_HEADS, HEAD_DIM, HEAD_DIM), F32)
        sout_spec = pl.BlockSpec((None, HEAD_DIM, HEAD_DIM), lambda h, g: (h, 0, 0))
        scratch.append(pltpu.VMEM((HEAD_DIM, HEAD_DIM), F32))
    else:
        in_specs.append(pl.BlockSpec((None, nseq, None, HEAD_DIM, HEAD_DIM), lambda h, g: (layer, g, h, 0, 0)))
        args.append(state)
        sout_shape = jax.ShapeDtypeStruct(state.shape[1:], F32)
        sout_spec = pl.BlockSpec((nseq, None, HEAD_DIM, HEAD_DIM), lambda h, g: (g, h, 0, 0))
    o, sout = pl.pallas_call(
        functools.partial(_retention_kernel, chunk=chunk, nseq=nseq, carry=carry),
        grid=(N_HEADS, ngroups),
        in_specs=in_specs,
        out_specs=[pl.BlockSpec(blk, lambda h, g: (g, h)), sout_spec],
        out_shape=[jax.ShapeDtypeStruct((t, GROUP_W), BF16), sout_shape],
        scratch_shapes=scratch,
        compiler_params=_cparams("arbitrary", "arbitrary"),
        name="retention",
    )(*args)
    return o, sout


def _strict_upper_ones(n):
    row = lax.broadcasted_iota(jnp.int32, (n, n), 0)
    col = lax.broadcasted_iota(jnp.int32, (n, n), 1)
    return jnp.where(row > col, 1.0, 0.0).astype(BF16)


def _sb_block(qb, kblk, vblk, carry, acc, *, diag):
    bq, bk = qb.shape[0], kblk.shape[0]
    z = _dot_nt(qb, kblk.astype(BF16)) * SB_SCALE
    lf = _log_sigmoid(-z)
    if diag:
        row = lax.broadcasted_iota(jnp.int32, (bq, bk), 0)
        col = lax.broadcasted_iota(jnp.int32, (bq, bk), 1)
        valid = col < row
        lf = jnp.where(valid, lf, 0.0)
    hi = lf.astype(BF16)
    lo = (lf - hi.astype(F32)).astype(BF16)
    ones_after = _strict_upper_ones(bk)
    later_in = _dot(hi, ones_after) + _dot(lo, ones_after)
    a = jnp.exp(z + lf + later_in + carry)
    if diag:
        a = jnp.where(valid, a, 0.0)
    acc = acc + _dot(a.astype(BF16), vblk.astype(BF16))
    carry = carry + later_in[:, 0:1] + lf[:, 0:1]
    return carry, acc


def _sb_kernel(q_ref, kn_ref, vn_ref, *rest, bq, bk, n_past_static):
    if n_past_static is None:
        (o_ref,) = rest
        kp_ref, vp_ref = kn_ref, vn_ref
        qi = pl.program_id(1)
        row0 = pl.multiple_of(qi * bq, bq)
        k_diag = kn_ref[pl.ds(row0, bq), :]
        v_diag = vn_ref[pl.ds(row0, bq), :]
        n_past = row0
    else:
        kp_ref, vp_ref, o_ref = rest
        k_diag = kn_ref[...]
        v_diag = vn_ref[...]
        n_past = n_past_static
    qb = q_ref[...].astype(BF16)
    carry0 = jnp.zeros((bq, 1), F32)
    acc0 = jnp.zeros((bq, HEAD_DIM), F32)
    carry, acc = _sb_block(qb, k_diag, v_diag, carry0, acc0, diag=True)
    n_blocks = n_past // bk

    def cond(st):
        c, carry, _ = st
        return jnp.logical_and(c < n_blocks, jnp.max(carry) > SB_DEAD_LOG)

    def body(st):
        c, carry, acc = st
        start = pl.multiple_of(n_past - (c + 1) * bk, bk)
        carry, acc = _sb_block(qb, kp_ref[pl.ds(start, bk), :], vp_ref[pl.ds(start, bk), :],
                               carry, acc, diag=False)
        return c + 1, carry, acc

    _, _, acc = lax.while_loop(cond, body, (jnp.int32(0), carry, acc))
    o_ref[...] = acc.astype(BF16)


def stick_breaking(z, k_past, v_past, layer, *, batch, seq):
    t = z.shape[0]
    if k_past is None:
        assert batch == 1
        bq = min(ROW_GROUP, t)
        bk = bq
        nq = t // bq
        kv_blk = (t, HEAD_DIM)
        in_specs = [
            pl.BlockSpec((bq, HEAD_DIM), _zcol(4)),
            pl.BlockSpec(kv_blk, lambda h, g: (0, 5 * N_HEADS + h)),
            pl.BlockSpec(kv_blk, lambda h, g: (0, 6 * N_HEADS + h)),
        ]
        args = [z, z, z]
        n_past_static = None
    else:
        bq = seq
        nq = batch
        past = k_past.shape[2]
        bk = min(ROW_GROUP, past)
        assert past % bk == 0
        past_spec = pl.BlockSpec((None, None, past, HEAD_DIM), lambda h, g: (layer, g, 0, h))
        in_specs = [
            pl.BlockSpec((bq, HEAD_DIM), _zcol(4)),
            pl.BlockSpec((bq, HEAD_DIM), _zcol(5)),
            pl.BlockSpec((bq, HEAD_DIM), _zcol(6)),
            past_spec, past_spec,
        ]
        args = [z, z, z, k_past, v_past]
        n_past_static = past
    return pl.pallas_call(
        functools.partial(_sb_kernel, bq=bq, bk=bk, n_past_static=n_past_static),
        grid=(N_HEADS, nq),
        in_specs=in_specs,
        out_specs=pl.BlockSpec((bq, HEAD_DIM), lambda h, g: (g, h)),
        out_shape=jax.ShapeDtypeStruct((t, GROUP_W), BF16),
        compiler_params=_cparams("arbitrary", "arbitrary"),
        name="stick_breaking",
    )(*args)


POOL_HEAD = 16


def _pool_kernel(u_ref, *rest, rows, pos0, carry):
    if carry:
        w_ref, sc_ref, o_ref, ext_ref = rest
        s0_ref = None
    else:
        s0_ref, w_ref, sc_ref, o_ref, ext_ref = rest
    g = pl.program_id(0)
    if carry:
        @pl.when(g == 0)
        def _():
            ext_ref[0:POOL_HEAD, :] = jnp.zeros((POOL_HEAD, GROUP_W), F32)
        base = g * rows + pos0
    else:
        ext_ref[0:POOL_HEAD, :] = s0_ref[...]
        base = pos0
    ext_ref[POOL_HEAD:POOL_HEAD + rows, :] = u_ref[...]
    pos1 = lax.broadcasted_iota(jnp.int32, (rows, HEAD_DIM), 0) + (base + 1)
    for gi, w in enumerate(POOL_WINDOWS):
        cols = slice(gi * HEAD_DIM, (gi + 1) * HEAD_DIM)
        tok = ext_ref[POOL_HEAD:POOL_HEAD + rows, cols]
        wsum = tok
        for i in range(1, w):
            wsum = wsum + ext_ref[POOL_HEAD - i:POOL_HEAD - i + rows, cols]
        cnt = jnp.minimum(pos1, w).astype(F32)
        p = wsum / cnt - tok
        y = _dot(p.astype(BF16), w_ref[gi]) * sc_ref[:, cols]
        o_ref[:, cols] = y.astype(BF16)
    if carry:
        ext_ref[0:POOL_HEAD, :] = ext_ref[rows:rows + POOL_HEAD, :]


def multi_pool(z, state, pool_w, pool_scale, layer, *, batch, seq, pos0):
    t = z.shape[0]
    carry = state is None
    rows = min(ROW_GROUP, t) if carry else seq
    in_specs = [pl.BlockSpec((rows, GROUP_W), lambda g: (g, 7))]
    args = [z]
    if not carry:
        in_specs.append(pl.BlockSpec((None, None, POOL_HEAD, GROUP_W), lambda g: (layer, g, 0, 0)))
        args.append(state)
    in_specs += [pl.BlockSpec((None,) + pool_w.shape[1:], lambda g: (layer, 0, 0, 0)),
                 pl.BlockSpec((1, GROUP_W), lambda g: (0, 0))]
    args += [pool_w, pool_scale.reshape(1, GROUP_W)]
    return pl.pallas_call(
        functools.partial(_pool_kernel, rows=rows, pos0=pos0, carry=carry),
        grid=(t // rows,),
        in_specs=in_specs,
        out_specs=pl.BlockSpec((rows, GROUP_W), lambda g: (g, 0)),
        out_shape=jax.ShapeDtypeStruct((t, GROUP_W), BF16),
        scratch_shapes=[pltpu.VMEM((POOL_HEAD + rows, GROUP_W), F32)],
        compiler_params=_cparams("arbitrary"),
        name="multi_pool",
    )(*args)


def _hgrn_tables(rows, chunk):
    nlev = int(math.log2(chunk))
    l = np.arange(rows)[:, None]
    m = np.arange(rows)[None, :]
    same = (l // chunk) == (m // chunk)
    x = np.bitwise_xor(l, m)
    hb = np.floor(np.log2(np.maximum(x, 1))).astype(np.int64)
    level = nlev - 1 - hb
    lv = np.where(same & (l > m), level, np.where(l == m, nlev, -1)).astype(np.int32)
    tri = (same & (l >= m)).astype(np.float32)
    return jnp.asarray(lv), jnp.asarray(tri, BF16)


def _split3(x):
    h1 = x.astype(BF16)
    r1 = x - h1.astype(F32)
    h2 = r1.astype(BF16)
    h3 = (r1 - h2.astype(F32)).astype(BF16)
    return h1, h2, h3


def _hgrn_kernel(q_ref, f_ref, i_ref, g_ref, lb_ref, gn_ref, lv_ref, tri_ref, *rest, chunk, nseq, carry):
    if carry:
        o_ref, sout_ref, gcum_ref, state_ref = rest
        s0_ref = None
    else:
        s0_ref, o_ref, sout_ref, gcum_ref = rest
        state_ref = None
    rows = chunk * nseq
    nlev = int(math.log2(chunk))
    g = pl.program_id(1)

    if carry:
        @pl.when(g == 0)
        def _():
            state_ref[...] = jnp.zeros_like(state_ref)

    gq = q_ref[...]
    q = gq * _sigmoid(gq)
    fl = f_ref[...]
    log_lb = lb_ref[0:1, :]
    log1m_lb = lb_ref[1:2, :]
    one_m_lb = lb_ref[2:3, :]
    b = log1m_lb + _log_sigmoid(fl)
    log_f = jnp.maximum(log_lb, b) + jnp.log1p(jnp.exp(-jnp.abs(log_lb - b)))
    kk = one_m_lb * (1.0 / (1.0 + jnp.exp(fl)))
    vb = i_ref[...].astype(BF16)

    tri = tri_ref[...]
    h1, h2, h3 = _split3(log_f)
    gcum = _dot(tri, h1) + _dot(tri, h2) + _dot(tri, h3)
    gcum_ref[...] = gcum

    ridx = lax.broadcasted_iota(jnp.int32, (rows, HEAD_DIM), 0)
    lv = lv_ref[...]
    qb = q.astype(BF16)
    kb = kk.astype(BF16)
    scores = jnp.where(lv == nlev, _dot_nt(qb, kb), 0.0)

    for level in range(nlev):
        blk = chunk >> level
        if blk < 16:
            break
        half = blk // 2
        pieces = [jnp.broadcast_to(gcum_ref[r0 + half - 1:r0 + half, :], (blk, HEAD_DIM))
                  for r0 in range(0, rows, blk)]
        gmid = pieces[0] if len(pieces) == 1 else jnp.concatenate(pieces, axis=0)
        upper = (ridx & (blk - 1)) >= half
        e = jnp.where(upper, gcum - gmid, gmid - gcum)
        w = jnp.exp(e)
        s = _dot_nt((q * w).astype(BF16), (kk * w).astype(BF16))
        scores = jnp.where(lv == level, s, scores)

    f1 = pltpu.roll(log_f, 1, 0)
    f2 = pltpu.roll(log_f, 2, 0)
    f3 = pltpu.roll(log_f, 3, 0)
    b1 = pltpu.roll(log_f, rows - 1, 0)
    b2 = pltpu.roll(log_f, rows - 2, 0)
    b3 = pltpu.roll(log_f, rows - 3, 0)
    a1 = log_f
    a2 = a1 + f1
    a3 = a2 + f2
    a4 = a3 + f3
    c1 = b1
    c2 = c1 + b2
    c3 = c2 + b3
    zero = jnp.zeros_like(log_f)

    def pick(idx, table):
        out = table[-1]
        for j in range(len(table) - 2, -1, -1):
            out = jnp.where(idx == j, table[j], out)
        return out

    small = {8: pick(ridx & 7, [c3, c2, c1, zero, a1, a2, a3, a4]),
             4: pick(ridx & 3, [c1, zero, a1, a2]),
             2: pick(ridx & 1, [zero, a1])}
    for blk in (8, 4, 2):
        if blk > chunk:
            continue
        level = nlev - int(math.log2(blk))
        w = jnp.exp(small[blk])
        s = _dot_nt((q * w).astype(BF16), (kk * w).astype(BF16))
        scores = jnp.where(lv == level, s, scores)

    o_intra = _dot(scores.astype(BF16), vb)
    qg = (q * jnp.exp(gcum)).astype(BF16)
    outs = []
    for bi in range(nseq):
        sl = slice(bi * chunk, (bi + 1) * chunk)
        glast = gcum_ref[(bi + 1) * chunk - 1:(bi + 1) * chunk, :]
        st_old = state_ref[...] if carry else s0_ref[bi].T
        outs.append(o_intra[sl] + _dot_nt(qg[sl], st_old.astype(BF16)))
        kdec = (kk[sl] * jnp.exp(glast - gcum[sl])).astype(BF16)
        st_new = st_old * jnp.exp(glast) + _dot(i_ref[sl, :].T.astype(BF16), kdec)
        if carry:
            state_ref[...] = st_new
        else:
            sout_ref[bi] = st_new.T
    o = outs[0] if nseq == 1 else jnp.concatenate(outs, axis=0)
    o_ref[...] = _head_norm_gate(o, gn_ref[...], _sigmoid(g_ref[...]))

    if carry:
        @pl.when(g == pl.num_programs(1) - 1)
        def _():
            sout_ref[...] = state_ref[...].T


def hgrn2(z, lb, gn, state, layer, *, batch, seq):
    t = z.shape[0]
    carry = state is None
    rows = min(ROW_GROUP, t)
    if carry:
        assert batch == 1 and t % rows == 0
        chunk, nseq = rows, 1
    else:
        chunk = seq
        nseq = rows // chunk
        assert rows % chunk == 0 and batch % nseq == 0
    ngroups = t // rows
    lv, tri = _hgrn_tables(rows, chunk)
    lbh = lb.reshape(N_HEADS, 1, HEAD_DIM)
    lb_tab = jnp.concatenate([jnp.log(lbh), jnp.log1p(-lbh), 1.0 - lbh], axis=1)
    blk = (rows, HEAD_DIM)
    const2 = lambda h, g: (0, 0)
    in_specs = [
        pl.BlockSpec(blk, _zcol(8)), pl.BlockSpec(blk, _zcol(9)),
        pl.BlockSpec(blk, _zcol(10)), pl.BlockSpec(blk, _zcol(11)),
        pl.BlockSpec((None, 3, HEAD_DIM), lambda h, g: (h, 0, 0)),
        pl.BlockSpec((None, 1, HEAD_DIM), lambda h, g: (h, 0, 0)),
        pl.BlockSpec((rows, rows), const2), pl.BlockSpec((rows, rows), const2),
    ]
    args = [z, z, z, z, lb_tab, gn.reshape(N_HEADS, 1, HEAD_DIM), lv, tri]
    scratch = [pltpu.VMEM(blk, F32)]
    if carry:
        sout_shape = jax.ShapeDtypeStruct((N_HEADS, HEAD_DIM, HEAD_DIM), F32)
        sout_spec = pl.BlockSpec((None, HEAD_DIM, HEAD_DIM), lambda h, g: (h, 0, 0))
        scratch.append(pltpu.VMEM((HEAD_DIM, HEAD_DIM), F32))
    else:
        in_specs.append(pl.BlockSpec((None, nseq, None, HEAD_DIM, HEAD_DIM), lambda h, g: (layer, g, h, 0, 0)))
        args.append(state)
        sout_shape = jax.ShapeDtypeStruct(state.shape[1:], F32)
        sout_spec = pl.BlockSpec((nseq, None, HEAD_DIM, HEAD_DIM), lambda h, g: (g, h, 0, 0))
    o, sout = pl.pallas_call(
        functools.partial(_hgrn_kernel, chunk=chunk, nseq=nseq, carry=carry),
        grid=(N_HEADS, ngroups),
        in_specs=in_specs,
        out_specs=[pl.BlockSpec(blk, lambda h, g: (g, h)), sout_spec],
        out_shape=[jax.ShapeDtypeStruct((t, GROUP_W), BF16), sout_shape],
        scratch_shapes=scratch,
        compiler_params=_cparams("arbitrary", "arbitrary"),
        name="hgrn2",
    )(*args)
    return o, sout


def _rope_tables(pos):
    half = HEAD_DIM // 2
    inv = ROPE_BASE ** (-jnp.arange(half, dtype=F32) / half)
    ang = pos.astype(F32)[:, None] * inv[None, :]
    cos = jnp.cos(ang)
    sin = jnp.sin(ang)
    return jnp.concatenate([cos, cos], axis=-1), jnp.concatenate([-sin, sin], axis=-1)


def _run_trunk(x, pos0, states, weights, lb_all):
    batch, seq, d = x.shape
    t = batch * seq
    depth = weights["w_in"].shape[0]
    x2 = x.reshape(t, d)
    fresh = states is None
    rows = min(ROW_GROUP, t)
    pos = pos0 + jnp.arange(seq, dtype=jnp.int32)
    cos, sin = _rope_tables(pos)
    if not fresh:
        reps = rows // seq
        cos, sin = jnp.tile(cos, (reps, 1)), jnp.tile(sin, (reps, 1))
    if fresh:
        ret_s = k_past = v_past = pool_s = hg_s = None
    else:
        ret_s, hg_s = states[0], states[4]
        past = states[1].shape[2]
        k_past = states[1].reshape(depth, batch, past, GROUP_W)
        v_past = states[2].reshape(depth, batch, past, GROUP_W)
        pool_s = jnp.pad(states[3], ((0, 0), (0, 0), (POOL_HEAD - POOL_PAD, 0), (0, 0)))
    rets, ks, vs, pools, hgs = [], [], [], [], []
    for l in range(depth):
        z = norm_matmul(x2, weights["norm1_g"][l], weights["w_in"], l)
        ro, ret_new = retention(z, cos, sin, weights["ret_norm_g"][l], ret_s, l, batch=batch, seq=seq)
        so = stick_breaking(z, k_past, v_past, l, batch=batch, seq=seq)
        po = multi_pool(z, pool_s, weights["pool_w"], weights["pool_scale"][l], l,
                        batch=batch, seq=seq, pos0=pos0)
        ho, hg_new = hgrn2(z, lb_all[l], weights["hg_norm_g"][l], hg_s, l, batch=batch, seq=seq)
        x2 = out_proj(x2, (ro, so, po, ho), weights["w_out"], l)
        x2 = mlp(x2, weights["norm2_g"][l], weights["w_up"], weights["w_down"], l)
        z3 = z.reshape(batch, seq, N_SLOTS * GROUP_W)
        rets.append(ret_new.reshape(batch, N_HEADS, HEAD_DIM, HEAD_DIM))
        ks.append(z3[:, :, 5 * GROUP_W:6 * GROUP_W].reshape(batch, seq, N_HEADS, HEAD_DIM))
        vs.append(z3[:, :, 6 * GROUP_W:7 * GROUP_W].reshape(batch, seq, N_HEADS, HEAD_DIM))
        pools.append(z3[:, seq - POOL_PAD:, 7 * GROUP_W:8 * GROUP_W])
        hgs.append(hg_new.reshape(batch, N_HEADS, HEAD_DIM, HEAD_DIM))
    y = final_norm(x2, weights["final_norm_g"]).reshape(batch, seq, d)
    return y, jnp.stack(rets), jnp.stack(ks), jnp.stack(vs), jnp.stack(pools), jnp.stack(hgs)


def kernel(x_prompt, x_sample, state_ret, cache_sb_k, cache_sb_v, state_pool, state_hgrn, norm1_g, w_in, ret_norm_g, pool_w, pool_scale, hg_lower_bounds, hg_norm_g, w_out, norm2_g, w_up, w_down, final_norm_g):
    lb_all = jnp.cumsum(jax.nn.softmax(hg_lower_bounds.astype(F32), axis=0), axis=0)
    lb_all = lb_all - lb_all[0:1]
    weights = dict(
        norm1_g=norm1_g, w_in=w_in.astype(BF16), ret_norm_g=ret_norm_g, pool_w=pool_w.astype(BF16),
        pool_scale=pool_scale, hg_norm_g=hg_norm_g, w_out=w_out.astype(BF16), norm2_g=norm2_g,
        w_up=w_up.astype(BF16), w_down=w_down.astype(BF16), final_norm_g=final_norm_g)
    past = cache_sb_k.shape[2]
    y_p, ret_p, sbk_p, sbv_p, pool_p, hg_p = _run_trunk(x_prompt, 0, None, weights, lb_all)
    y_s, ret_s, sbk_s, sbv_s, pool_s, hg_s = _run_trunk(
        x_sample, past, (state_ret, cache_sb_k, cache_sb_v, state_pool, state_hgrn), weights, lb_all)
    return (y_p, y_s, ret_p, ret_s, sbk_p, sbv_p, sbk_s, sbv_s, pool_p, pool_s, hg_p, hg_s)
```

```python
import functools
import math

import numpy as np
import jax
import jax.numpy as jnp
from jax import lax
from jax.experimental import pallas as pl
from jax.experimental.pallas import tpu as pltpu

F32 = jnp.float32
BF16 = jnp.bfloat16

HEAD_DIM = 128
N_HEADS = 4
GROUP_W = N_HEADS * HEAD_DIM
N_SLOTS = 12
POOL_WINDOWS = (2, 4, 8, 16)
POOL_PAD = max(POOL_WINDOWS) - 1
ROPE_BASE = 10000.0
EPS = 1e-6
SB_SCALE = HEAD_DIM ** -0.5
SB_DEAD_LOG = -104.0

VMEM_LIMIT = 56 * 1024 * 1024
ROW_GROUP = 256


def _cparams(*sem):
    return pltpu.CompilerParams(dimension_semantics=sem, vmem_limit_bytes=VMEM_LIMIT)


def _dot(a, b):
    return jnp.dot(a, b, preferred_element_type=F32)


def _dot_nt(a, b):
    return lax.dot_general(a, b, (((1,), (1,)), ((), ())), preferred_element_type=F32)


def _sigmoid(x):
    return 1.0 / (1.0 + jnp.exp(-x))


def _log_sigmoid(x):
    return jnp.minimum(x, 0.0) - jnp.log1p(jnp.exp(-jnp.abs(x)))


def _rms(x, g):
    ms = jnp.mean(x * x, axis=-1, keepdims=True)
    return x * lax.rsqrt(ms + EPS) * g


def _hcols(h):
    return slice(h * HEAD_DIM, (h + 1) * HEAD_DIM)


def _norm_matmul_kernel(x_ref, g_ref, w_ref, o_ref):
    h = _rms(x_ref[...], g_ref[...]).astype(BF16)
    o_ref[...] = _dot(h, w_ref[...])


def norm_matmul(x, g, w, layer, *, tm=256):
    t, d = x.shape
    n = w.shape[2]
    tm = min(tm, t)
    return pl.pallas_call(
        _norm_matmul_kernel,
        grid=(t // tm,),
        in_specs=[
            pl.BlockSpec((tm, d), lambda i: (i, 0)),
            pl.BlockSpec((1, d), lambda i: (0, 0)),
            pl.BlockSpec((None, d, n), lambda i: (layer, 0, 0), pipeline_mode=pl.Buffered(1)),
        ],
        out_specs=pl.BlockSpec((tm, n), lambda i: (i, 0)),
        out_shape=jax.ShapeDtypeStruct((t, n), F32),
        compiler_params=_cparams("parallel"),
        name="norm_in_proj",
    )(x, g.reshape(1, d), w)


def _out_proj_kernel(x_ref, m0_ref, m1_ref, m2_ref, m3_ref, w_ref, o_ref):
    acc = x_ref[...]
    for gi, m_ref in enumerate((m0_ref, m1_ref, m2_ref, m3_ref)):
        acc = acc + _dot(m_ref[...], w_ref[gi * GROUP_W:(gi + 1) * GROUP_W, :])
    o_ref[...] = acc


def out_proj(x, mixes, w, layer, *, tm=512):
    t, d = x.shape
    tm = min(tm, t)
    mix_spec = pl.BlockSpec((tm, GROUP_W), lambda i: (i, 0))
    return pl.pallas_call(
        _out_proj_kernel,
        grid=(t // tm,),
        in_specs=[pl.BlockSpec((tm, d), lambda i: (i, 0)), mix_spec, mix_spec, mix_spec, mix_spec,
                  pl.BlockSpec((None,) + w.shape[1:], lambda i: (layer, 0, 0), pipeline_mode=pl.Buffered(1))],
        out_specs=pl.BlockSpec((tm, d), lambda i: (i, 0)),
        out_shape=jax.ShapeDtypeStruct((t, d), F32),
        compiler_params=_cparams("parallel"),
        name="out_proj",
    )(x, *mixes, w)


def _mlp_kernel(x_ref, g_ref, wu_ref, wd_ref, o_ref, h_ref):
    @pl.when(pl.program_id(1) == 0)
    def _():
        x = x_ref[...]
        h_ref[...] = _rms(x, g_ref[...]).astype(BF16)
        o_ref[...] = x

    u = _dot(h_ref[...], wu_ref[...])
    a = jnp.square(jnp.maximum(u, 0.0)).astype(BF16)
    o_ref[...] += _dot(a, wd_ref[...])


def mlp(x, g, w_up, w_down, layer, *, tm=1024, tf=512):
    t, d = x.shape
    dff = w_up.shape[2]
    tm = min(tm, t)
    return pl.pallas_call(
        _mlp_kernel,
        grid=(t // tm, dff // tf),
        in_specs=[
            pl.BlockSpec((tm, d), lambda i, f: (i, 0)),
            pl.BlockSpec((1, d), lambda i, f: (0, 0)),
            pl.BlockSpec((None, d, tf), lambda i, f: (layer, 0, f)),
            pl.BlockSpec((None, tf, d), lambda i, f: (layer, f, 0)),
        ],
        out_specs=pl.BlockSpec((tm, d), lambda i, f: (i, 0)),
        out_shape=jax.ShapeDtypeStruct((t, d), F32),
        scratch_shapes=[pltpu.VMEM((tm, d), BF16)],
        compiler_params=_cparams("parallel", "arbitrary"),
        name="mlp",
    )(x, g.reshape(1, d), w_up, w_down)


def _final_norm_kernel(x_ref, g_ref, o_ref):
    o_ref[...] = _rms(x_ref[...], g_ref[...])


def final_norm(x, g, *, tm=512):
    t, d = x.shape
    tm = min(tm, t)
    return pl.pallas_call(
        _final_norm_kernel,
        grid=(t // tm,),
        in_specs=[pl.BlockSpec((tm, d), lambda i: (i, 0)), pl.BlockSpec((1, d), lambda i: (0, 0))],
        out_specs=pl.BlockSpec((tm, d), lambda i: (i, 0)),
        out_shape=jax.ShapeDtypeStruct((t, d), F32),
        compiler_params=_cparams("parallel"),
        name="final_norm",
    )(x, g.reshape(1, d))


def _zslot(slot):
    return lambda g: (g, slot)


def _head_norm_gate(o, gn, gate):
    return (_rms(o, gn) * gate).astype(BF16)


def _state_specs(state, layer, nseq):
    blk = (nseq, N_HEADS, HEAD_DIM, HEAD_DIM)
    in_spec = pl.BlockSpec((None,) + blk, lambda g: (layer, g, 0, 0, 0))
    out_spec = pl.BlockSpec(blk, lambda g: (g, 0, 0, 0))
    return in_spec, out_spec, jax.ShapeDtypeStruct(state.shape[1:], F32)


_FRESH_STATE_SHAPE = jax.ShapeDtypeStruct((N_HEADS, HEAD_DIM, HEAD_DIM), F32)
_FRESH_STATE_SPEC = pl.BlockSpec((N_HEADS, HEAD_DIM, HEAD_DIM), lambda g: (0, 0, 0))


def _group_shape(t, batch, seq, fresh):
    rows = min(ROW_GROUP, t)
    if fresh:
        assert batch == 1 and t % rows == 0
        return rows, rows, 1
    assert rows % seq == 0 and batch % (rows // seq) == 0
    return rows, seq, rows // seq


def _retention_kernel(lg_ref, q_ref, k_ref, v_ref, g_ref, cos_ref, sin_ref, gn_ref, *rest,
                      chunk, nseq, carry):
    if carry:
        o_ref, sout_ref, dmask_ref, qdec_ref, kdec_ref, state_ref = rest
        s0_ref = None
    else:
        s0_ref, o_ref, sout_ref, dmask_ref, qdec_ref, kdec_ref = rest
        state_ref = None
    rows = chunk * nseq
    g = pl.program_id(0)

    @pl.when(g == 0)
    def _():
        row = lax.broadcasted_iota(jnp.int32, (rows, rows), 0)
        col = lax.broadcasted_iota(jnp.int32, (rows, rows), 1)
        diff = row - col
        ok = jnp.where((row ^ col) < chunk, diff, -1) >= 0
        dist = jnp.maximum(diff, 0).astype(F32)
        pos = (lax.broadcasted_iota(jnp.int32, (rows, HEAD_DIM), 0) & (chunk - 1)).astype(F32)
        for h in range(N_HEADS):
            lg = lg_ref[h]
            dmask_ref[h] = jnp.where(ok, jnp.exp(dist * lg), 0.0)
            qdec_ref[h] = jnp.exp((pos + 1.0) * lg)
            kdec_ref[h] = jnp.exp((chunk - 1.0 - pos) * lg)
        if carry:
            state_ref[...] = jnp.zeros_like(state_ref)

    cos = cos_ref[...]
    sin = sin_ref[...]

    def rope(x):
        return x * cos + pltpu.roll(x, HEAD_DIM // 2, 1) * sin

    for h in range(N_HEADS):
        hc = _hcols(h)
        q = rope(q_ref[:, hc])
        k = rope(k_ref[:, hc]) * SB_SCALE
        vb = v_ref[:, hc].astype(BF16)
        scores = _dot_nt(q.astype(BF16), k.astype(BF16)) * dmask_ref[h]
        o_intra = _dot(scores.astype(BF16), vb)
        qd = (q * qdec_ref[h]).astype(BF16)
        kd = k * kdec_ref[h]
        cdec = qdec_ref[h, chunk - 1:chunk, :]
        outs = []
        for b in range(nseq):
            sl = slice(b * chunk, (b + 1) * chunk)
            s_old = state_ref[h] if carry else s0_ref[b, h]
            outs.append(o_intra[sl] + _dot(qd[sl], s_old.astype(BF16)))
            s_new = cdec * s_old + _dot(kd[sl].T.astype(BF16), vb[sl])
            if carry:
                state_ref[h] = s_new
            else:
                sout_ref[b, h] = s_new
        o = outs[0] if nseq == 1 else jnp.concatenate(outs, axis=0)
        gate = g_ref[:, hc]
        o_ref[:, hc] = _head_norm_gate(o, gn_ref[:, hc], gate * _sigmoid(gate))

    if carry:
        @pl.when(g == pl.num_programs(0) - 1)
        def _():
            sout_ref[...] = state_ref[...]


def retention(z, cos, sin, gn, state, layer, *, batch, seq):
    t = z.shape[0]
    carry = state is None
    rows, chunk, nseq = _group_shape(t, batch, seq, carry)
    lg = jnp.asarray(np.log(1.0 - 2.0 ** (-5.0 - np.arange(N_HEADS))), F32)
    blk = (rows, GROUP_W)
    tab_blk = (rows, HEAD_DIM)
    tab_idx = (lambda g: (g, 0)) if carry else (lambda g: (0, 0))
    in_specs = [
        pl.BlockSpec(memory_space=pltpu.SMEM),
        pl.BlockSpec(blk, _zslot(0)), pl.BlockSpec(blk, _zslot(1)),
        pl.BlockSpec(blk, _zslot(2)), pl.BlockSpec(blk, _zslot(3)),
        pl.BlockSpec(tab_blk, tab_idx), pl.BlockSpec(tab_blk, tab_idx),
        pl.BlockSpec((1, GROUP_W), lambda g: (0, 0)),
    ]
    args = [lg, z, z, z, z, cos, sin, gn.reshape(1, GROUP_W)]
    scratch = [pltpu.VMEM((N_HEADS, rows, rows), F32), pltpu.VMEM((N_HEADS,) + tab_blk, F32),
               pltpu.VMEM((N_HEADS,) + tab_blk, F32)]
    if carry:
        sout_shape, sout_spec = _FRESH_STATE_SHAPE, _FRESH_STATE_SPEC
        scratch.append(pltpu.VMEM((N_HEADS, HEAD_DIM, HEAD_DIM), F32))
    else:
        st_spec, sout_spec, sout_shape = _state_specs(state, layer, nseq)
        in_specs.append(st_spec)
        args.append(state)
    o, sout = pl.pallas_call(
        functools.partial(_retention_kernel, chunk=chunk, nseq=nseq, carry=carry),
        grid=(t // rows,),
        in_specs=in_specs,
        out_specs=[pl.BlockSpec(blk, lambda g: (g, 0)), sout_spec],
        out_shape=[jax.ShapeDtypeStruct((t, GROUP_W), BF16), sout_shape],
        scratch_shapes=scratch,
        compiler_params=_cparams("arbitrary"),
        name="retention",
    )(*args)
    return o, sout


def _strict_upper_ones(n):
    row = lax.broadcasted_iota(jnp.int32, (n, n), 0)
    col = lax.broadcasted_iota(jnp.int32, (n, n), 1)
    return jnp.where(row > col, 1.0, 0.0).astype(BF16)


def _sb_block(qb, kblk, vblk, carry, acc, *, diag):
    bq, bk = qb.shape[0], kblk.shape[0]
    z = _dot_nt(qb, kblk.astype(BF16)) * SB_SCALE
    lf = _log_sigmoid(-z)
    if diag:
        row = lax.broadcasted_iota(jnp.int32, (bq, bk), 0)
        col = lax.broadcasted_iota(jnp.int32, (bq, bk), 1)
        valid = col < row
        lf = jnp.where(valid, lf, 0.0)
    hi = lf.astype(BF16)
    lo = (lf - hi.astype(F32)).astype(BF16)
    ones_after = _strict_upper_ones(bk)
    later_in = _dot(hi, ones_after) + _dot(lo, ones_after)
    a = jnp.exp(z + lf + later_in + carry)
    if diag:
        a = jnp.where(valid, a, 0.0)
    acc = acc + _dot(a.astype(BF16), vblk.astype(BF16))
    carry = carry + later_in[:, 0:1] + lf[:, 0:1]
    return carry, acc


def _sb_kernel(q_ref, kd_ref, vd_ref, kp_ref, vp_ref, kfar_ref, vfar_ref, o_ref, kbuf_ref, vbuf_ref, sem_ref,
               *, bq, bk, fresh, layer, past):
    g = pl.program_id(0)
    qs = [q_ref[:, _hcols(h)].astype(BF16) for h in range(N_HEADS)]

    if fresh:
        def head_rows(ref, h):
            return ref[:, _hcols(h)]
        n_blocks = g
    else:
        def head_rows(ref, h):
            return ref[pl.ds(h, bk, stride=N_HEADS), :]
        n_blocks = past // bk

    def sweep(carries, accs, k_ref, v_ref, get, diag):
        out = [_sb_block(qs[h], get(k_ref, h), get(v_ref, h), carries[h], accs[h], diag=diag)
               for h in range(N_HEADS)]
        return tuple(c for c, _ in out), tuple(a for _, a in out)

    def alive(carries):
        m = functools.reduce(jnp.maximum, carries)
        return jnp.max(m) > SB_DEAD_LOG

    carries = tuple(jnp.zeros((bq, 1), F32) for _ in range(N_HEADS))
    accs = tuple(jnp.zeros((bq, HEAD_DIM), F32) for _ in range(N_HEADS))
    carries, accs = sweep(carries, accs, kd_ref, vd_ref, lambda ref, h: ref[:, _hcols(h)], True)

    carries, accs = lax.cond(
        jnp.logical_and(n_blocks >= 1, alive(carries)),
        lambda c, a: sweep(c, a, kp_ref, vp_ref, head_rows, False),
        lambda c, a: (c, a),
        carries, accs)

    def cond(st):
        c, carries, _ = st
        return jnp.logical_and(c < n_blocks, alive(carries))

    def body(st):
        c, carries, accs = st
        if fresh:
            start = pl.multiple_of((g - 1 - c) * bk, bk)
            k_src = kfar_ref.at[pl.ds(start, bk), pl.ds(5 * GROUP_W, GROUP_W)]
            v_src = vfar_ref.at[pl.ds(start, bk), pl.ds(6 * GROUP_W, GROUP_W)]
        else:
            start = pl.multiple_of((past - (c + 1) * bk) * N_HEADS, bk * N_HEADS)
            k_src = kfar_ref.at[layer, g, pl.ds(start, bk * N_HEADS), :]
            v_src = vfar_ref.at[layer, g, pl.ds(start, bk * N_HEADS), :]
        k_copy = pltpu.make_async_copy(k_src, kbuf_ref, sem_ref.at[0])
        v_copy = pltpu.make_async_copy(v_src, vbuf_ref, sem_ref.at[1])
        k_copy.start()
        v_copy.start()
        k_copy.wait()
        v_copy.wait()
        carries, accs = sweep(carries, accs, kbuf_ref, vbuf_ref, head_rows, False)
        return c + 1, carries, accs

    _, _, accs = lax.while_loop(cond, body, (jnp.int32(1), carries, accs))
    for h in range(N_HEADS):
        o_ref[:, _hcols(h)] = accs[h].astype(BF16)


def stick_breaking(z, k_past, v_past, layer, *, batch, seq):
    t = z.shape[0]
    fresh = k_past is None
    if fresh:
        assert batch == 1
        bq = bk = min(ROW_GROUP, t)
        past = None
        prev_idx = lambda slot: (lambda g: (jnp.maximum(g - 1, 0), slot))
        past_specs = [pl.BlockSpec((bk, GROUP_W), prev_idx(5)), pl.BlockSpec((bk, GROUP_W), prev_idx(6))]
        k_far, v_far = z, z
        buf_shape = (bk, GROUP_W)
    else:
        bq = seq
        past = k_past.shape[2] // N_HEADS
        bk = min(ROW_GROUP, past)
        assert past % bk == 0
        last = past // bk - 1
        tail = pl.BlockSpec((None, None, bk * N_HEADS, HEAD_DIM), lambda g: (layer, g, last, 0))
        past_specs = [tail, tail]
        k_far, v_far = k_past, v_past
        buf_shape = (bk * N_HEADS, HEAD_DIM)
    blk = (bq, GROUP_W)
    any_spec = pl.BlockSpec(memory_space=pl.ANY)
    return pl.pallas_call(
        functools.partial(_sb_kernel, bq=bq, bk=bk, fresh=fresh, layer=layer, past=past),
        grid=(t // bq,),
        in_specs=[pl.BlockSpec(blk, _zslot(4)), pl.BlockSpec(blk, _zslot(5)), pl.BlockSpec(blk, _zslot(6)),
                  *past_specs, any_spec, any_spec],
        out_specs=pl.BlockSpec(blk, lambda g: (g, 0)),
        out_shape=jax.ShapeDtypeStruct((t, GROUP_W), BF16),
        scratch_shapes=[pltpu.VMEM(buf_shape, F32), pltpu.VMEM(buf_shape, F32), pltpu.SemaphoreType.DMA((2,))],
        compiler_params=_cparams("arbitrary"),
        name="stick_breaking",
    )(z, z, z, k_far, v_far, k_far, v_far)


def _emit_kv_kernel(*refs, depth, rows):
    k_refs, v_refs = refs[:depth], refs[depth:2 * depth]
    ko_ref, vo_ref = refs[2 * depth:]
    for l in range(depth):
        for h in range(N_HEADS):
            ko_ref[l, pl.ds(h, rows, stride=N_HEADS), :] = k_refs[l][:, _hcols(h)]
            vo_ref[l, pl.ds(h, rows, stride=N_HEADS), :] = v_refs[l][:, _hcols(h)]


def emit_kv(zs, *, rows=ROW_GROUP):
    depth = len(zs)
    t = zs[0].shape[0]
    rows = min(rows, t)
    out_shape = jax.ShapeDtypeStruct((depth, t * N_HEADS, HEAD_DIM), F32)
    out_spec = pl.BlockSpec((depth, rows * N_HEADS, HEAD_DIM), lambda g: (0, g, 0))
    blk = (rows, GROUP_W)
    return pl.pallas_call(
        functools.partial(_emit_kv_kernel, depth=depth, rows=rows),
        grid=(t // rows,),
        in_specs=[pl.BlockSpec(blk, _zslot(5))] * depth + [pl.BlockSpec(blk, _zslot(6))] * depth,
        out_specs=[out_spec, out_spec],
        out_shape=[out_shape, out_shape],
        compiler_params=_cparams("parallel"),
        name="emit_kv",
    )(*zs, *zs)


POOL_HEAD = 16


def _pool_kernel(u_ref, *rest, rows, pos0, carry):
    if carry:
        w_ref, sc_ref, o_ref, ext_ref = rest
        s0_ref = None
    else:
        s0_ref, w_ref, sc_ref, o_ref, ext_ref = rest
    g = pl.program_id(0)
    if carry:
        @pl.when(g == 0)
        def _():
            ext_ref[0:POOL_HEAD, :] = jnp.zeros((POOL_HEAD, GROUP_W), F32)
        base = g * rows + pos0
    else:
        ext_ref[0:POOL_HEAD, :] = s0_ref[...]
        base = pos0
    ext_ref[POOL_HEAD:POOL_HEAD + rows, :] = u_ref[...]
    pos1 = lax.broadcasted_iota(jnp.int32, (rows, HEAD_DIM), 0) + (base + 1)
    for gi, w in enumerate(POOL_WINDOWS):
        cols = _hcols(gi)
        tok = ext_ref[POOL_HEAD:POOL_HEAD + rows, cols]
        wsum = tok
        for i in range(1, w):
            wsum = wsum + ext_ref[POOL_HEAD - i:POOL_HEAD - i + rows, cols]
        cnt = jnp.minimum(pos1, w).astype(F32)
        p = wsum / cnt - tok
        y = _dot(p.astype(BF16), w_ref[gi]) * sc_ref[:, cols]
        o_ref[:, cols] = y.astype(BF16)
    if carry:
        ext_ref[0:POOL_HEAD, :] = ext_ref[rows:rows + POOL_HEAD, :]


def multi_pool(z, state, pool_w, pool_scale, layer, *, batch, seq, pos0):
    t = z.shape[0]
    carry = state is None
    rows = min(ROW_GROUP, t) if carry else seq
    in_specs = [pl.BlockSpec((rows, GROUP_W), _zslot(7))]
    args = [z]
    if not carry:
        in_specs.append(pl.BlockSpec((None, None, POOL_HEAD, GROUP_W), lambda g: (layer, g, 0, 0)))
        args.append(state)
    in_specs += [pl.BlockSpec((None,) + pool_w.shape[1:], lambda g: (layer, 0, 0, 0)),
                 pl.BlockSpec((1, GROUP_W), lambda g: (0, 0))]
    args += [pool_w, pool_scale.reshape(1, GROUP_W)]
    return pl.pallas_call(
        functools.partial(_pool_kernel, rows=rows, pos0=pos0, carry=carry),
        grid=(t // rows,),
        in_specs=in_specs,
        out_specs=pl.BlockSpec((rows, GROUP_W), lambda g: (g, 0)),
        out_shape=jax.ShapeDtypeStruct((t, GROUP_W), BF16),
        scratch_shapes=[pltpu.VMEM((POOL_HEAD + rows, GROUP_W), F32)],
        compiler_params=_cparams("arbitrary"),
        name="multi_pool",
    )(*args)


def _hgrn_tables(rows, chunk):
    nlev = int(math.log2(chunk))
    l = np.arange(rows)[:, None]
    m = np.arange(rows)[None, :]
    same = (l // chunk) == (m // chunk)
    x = np.bitwise_xor(l, m)
    hb = np.floor(np.log2(np.maximum(x, 1))).astype(np.int64)
    level = nlev - 1 - hb
    lv = np.where(same & (l > m), level, np.where(l == m, nlev, -1)).astype(np.int32)
    tri = (same & (l >= m)).astype(np.float32)
    return jnp.asarray(lv), jnp.asarray(tri, BF16)


def _split3(x):
    h1 = x.astype(BF16)
    r1 = x - h1.astype(F32)
    h2 = r1.astype(BF16)
    h3 = (r1 - h2.astype(F32)).astype(BF16)
    return h1, h2, h3


def _hgrn_head(gq, fl, vi, gate, lb, gn, lv, tri, gcum_ref, state_in, chunk, nseq):
    rows = chunk * nseq
    nlev = int(math.log2(chunk))
    q = gq * _sigmoid(gq)
    log_lb, log1m_lb, one_m_lb = lb[0:1, :], lb[1:2, :], lb[2:3, :]
    b = log1m_lb + _log_sigmoid(fl)
    log_f = jnp.maximum(log_lb, b) + jnp.log1p(jnp.exp(-jnp.abs(log_lb - b)))
    kk = one_m_lb * (1.0 / (1.0 + jnp.exp(fl)))
    vb = vi.astype(BF16)

    h1, h2, h3 = _split3(log_f)
    gcum = _dot(tri, h1) + _dot(tri, h2) + _dot(tri, h3)
    gcum_ref[...] = gcum

    ridx = lax.broadcasted_iota(jnp.int32, (rows, HEAD_DIM), 0)
    scores = jnp.where(lv == nlev, _dot_nt(q.astype(BF16), kk.astype(BF16)), 0.0)

    def add_level(scores, level, e):
        w = jnp.exp(e)
        s = _dot_nt((q * w).astype(BF16), (kk * w).astype(BF16))
        return jnp.where(lv == level, s, scores)

    for level in range(nlev):
        blk = chunk >> level
        if blk < 16:
            break
        half = blk // 2
        pieces = [jnp.broadcast_to(gcum_ref[r0 + half - 1:r0 + half, :], (blk, HEAD_DIM))
                  for r0 in range(0, rows, blk)]
        gmid = pieces[0] if len(pieces) == 1 else jnp.concatenate(pieces, axis=0)
        scores = add_level(scores, level, -jnp.abs(gcum - gmid))

    f1 = pltpu.roll(log_f, 1, 0)
    f2 = pltpu.roll(log_f, 2, 0)
    f3 = pltpu.roll(log_f, 3, 0)
    b1 = pltpu.roll(log_f, rows - 1, 0)
    b2 = pltpu.roll(log_f, rows - 2, 0)
    b3 = pltpu.roll(log_f, rows - 3, 0)
    a1 = log_f
    a2 = a1 + f1
    a3 = a2 + f2
    a4 = a3 + f3
    c1 = b1
    c2 = c1 + b2
    c3 = c2 + b3
    zero = jnp.zeros_like(log_f)

    def pick(idx, table):
        out = table[-1]
        for j in range(len(table) - 2, -1, -1):
            out = jnp.where(idx == j, table[j], out)
        return out

    small = {8: pick(ridx & 7, [c3, c2, c1, zero, a1, a2, a3, a4]),
             4: pick(ridx & 3, [c1, zero, a1, a2]),
             2: pick(ridx & 1, [zero, a1])}
    for blk in (8, 4, 2):
        if blk <= chunk:
            scores = add_level(scores, nlev - int(math.log2(blk)), small[blk])

    o_intra = _dot(scores.astype(BF16), vb)
    qg = (q * jnp.exp(gcum)).astype(BF16)
    outs, states = [], []
    for bi in range(nseq):
        sl = slice(bi * chunk, (bi + 1) * chunk)
        glast = gcum_ref[(bi + 1) * chunk - 1:(bi + 1) * chunk, :]
        st_old = state_in(bi)
        outs.append(o_intra[sl] + _dot_nt(qg[sl], st_old.astype(BF16)))
        kdec = (kk[sl] * jnp.exp(glast - gcum[sl])).astype(BF16)
        states.append(st_old * jnp.exp(glast) + _dot(vi[sl].T.astype(BF16), kdec))
    o = outs[0] if nseq == 1 else jnp.concatenate(outs, axis=0)
    return _head_norm_gate(o, gn, _sigmoid(gate)), states


def _hgrn_kernel(q_ref, f_ref, i_ref, g_ref, lb_ref, gn_ref, lv_ref, tri_ref, *rest, chunk, nseq, carry):
    if carry:
        o_ref, sout_ref, gcum_ref, state_ref = rest
        s0_ref = None
    else:
        s0_ref, o_ref, sout_ref, gcum_ref = rest
        state_ref = None
    g = pl.program_id(0)

    if carry:
        @pl.when(g == 0)
        def _():
            state_ref[...] = jnp.zeros_like(state_ref)

    lv = lv_ref[...]
    tri = tri_ref[...]
    for h in range(N_HEADS):
        hc = _hcols(h)
        if carry:
            state_in = lambda bi, h=h: state_ref[h]
        else:
            state_in = lambda bi, h=h: s0_ref[bi, h].T
        o, states = _hgrn_head(q_ref[:, hc], f_ref[:, hc], i_ref[:, hc], g_ref[:, hc], lb_ref[:, hc],
                               gn_ref[:, hc], lv, tri, gcum_ref.at[h], state_in, chunk, nseq)
        o_ref[:, hc] = o
        for bi, st in enumerate(states):
            if carry:
                state_ref[h] = st
            else:
                sout_ref[bi, h] = st.T

    if carry:
        @pl.when(g == pl.num_programs(0) - 1)
        def _():
            for h in range(N_HEADS):
                sout_ref[h] = state_ref[h].T


def hgrn2(z, lb, gn, state, layer, *, batch, seq):
    t = z.shape[0]
    carry = state is None
    rows, chunk, nseq = _group_shape(t, batch, seq, carry)
    lv, tri = _hgrn_tables(rows, chunk)
    lb_tab = jnp.stack([jnp.log(lb), jnp.log1p(-lb), 1.0 - lb], axis=0)
    blk = (rows, GROUP_W)
    const2 = lambda g: (0, 0)
    in_specs = [
        pl.BlockSpec(blk, _zslot(8)), pl.BlockSpec(blk, _zslot(9)),
        pl.BlockSpec(blk, _zslot(10)), pl.BlockSpec(blk, _zslot(11)),
        pl.BlockSpec((3, GROUP_W), const2), pl.BlockSpec((1, GROUP_W), const2),
        pl.BlockSpec((rows, rows), const2), pl.BlockSpec((rows, rows), const2),
    ]
    args = [z, z, z, z, lb_tab, gn.reshape(1, GROUP_W), lv, tri]
    scratch = [pltpu.VMEM((N_HEADS, rows, HEAD_DIM), F32)]
    if carry:
        sout_shape, sout_spec = _FRESH_STATE_SHAPE, _FRESH_STATE_SPEC
        scratch.append(pltpu.VMEM((N_HEADS, HEAD_DIM, HEAD_DIM), F32))
    else:
        st_spec, sout_spec, sout_shape = _state_specs(state, layer, nseq)
        in_specs.append(st_spec)
        args.append(state)
    o, sout = pl.pallas_call(
        functools.partial(_hgrn_kernel, chunk=chunk, nseq=nseq, carry=carry),
        grid=(t // rows,),
        in_specs=in_specs,
        out_specs=[pl.BlockSpec(blk, lambda g: (g, 0)), sout_spec],
        out_shape=[jax.ShapeDtypeStruct((t, GROUP_W), BF16), sout_shape],
        scratch_shapes=scratch,
        compiler_params=_cparams("arbitrary"),
        name="hgrn2",
    )(*args)
    return o, sout


def _rope_tables(pos):
    half = HEAD_DIM // 2
    inv = ROPE_BASE ** (-jnp.arange(half, dtype=F32) / half)
    ang = pos.astype(F32)[:, None] * inv[None, :]
    cos = jnp.cos(ang)
    sin = jnp.sin(ang)
    return jnp.concatenate([cos, cos], axis=-1), jnp.concatenate([-sin, sin], axis=-1)


def _run_trunk(x, pos0, states, weights, lb_all):
    batch, seq, d = x.shape
    t = batch * seq
    depth = weights["w_in"].shape[0]
    x2 = x.reshape(t, d)
    fresh = states is None
    rows = min(ROW_GROUP, t)
    pos = pos0 + jnp.arange(seq, dtype=jnp.int32)
    cos, sin = _rope_tables(pos)
    if fresh:
        ret_s = k_past = v_past = pool_s = hg_s = None
    else:
        reps = rows // seq
        cos, sin = jnp.tile(cos, (reps, 1)), jnp.tile(sin, (reps, 1))
        ret_s, hg_s = states[0], states[4]
        past = states[1].shape[2]
        k_past = states[1].reshape(depth, batch, past * N_HEADS, HEAD_DIM)
        v_past = states[2].reshape(depth, batch, past * N_HEADS, HEAD_DIM)
        pool_s = jnp.pad(states[3], ((0, 0), (0, 0), (POOL_HEAD - POOL_PAD, 0), (0, 0)))
    zs, rets, pools, hgs = [], [], [], []
    for l in range(depth):
        z = norm_matmul(x2, weights["norm1_g"][l], weights["w_in"], l)
        ro, ret_new = retention(z, cos, sin, weights["ret_norm_g"][l], ret_s, l, batch=batch, seq=seq)
        so = stick_breaking(z, k_past, v_past, l, batch=batch, seq=seq)
        po = multi_pool(z, pool_s, weights["pool_w"], weights["pool_scale"][l], l,
                        batch=batch, seq=seq, pos0=pos0)
        ho, hg_new = hgrn2(z, lb_all[l], weights["hg_norm_g"][l], hg_s, l, batch=batch, seq=seq)
        x2 = out_proj(x2, (ro, so, po, ho), weights["w_out"], l)
        x2 = mlp(x2, weights["norm2_g"][l], weights["w_up"], weights["w_down"], l)
        zs.append(z)
        rets.append(ret_new.reshape(batch, N_HEADS, HEAD_DIM, HEAD_DIM))
        pools.append(z.reshape(batch, seq, N_SLOTS * GROUP_W)[:, seq - POOL_PAD:, 7 * GROUP_W:8 * GROUP_W])
        hgs.append(hg_new.reshape(batch, N_HEADS, HEAD_DIM, HEAD_DIM))
    y = final_norm(x2, weights["final_norm_g"]).reshape(batch, seq, d)
    ks, vs = emit_kv(zs)
    kv_shape = (depth, batch, seq, N_HEADS, HEAD_DIM)
    return y, jnp.stack(rets), ks.reshape(kv_shape), vs.reshape(kv_shape), jnp.stack(pools), jnp.stack(hgs)


def kernel(x_prompt, x_sample, state_ret, cache_sb_k, cache_sb_v, state_pool, state_hgrn, norm1_g, w_in, ret_norm_g, pool_w, pool_scale, hg_lower_bounds, hg_norm_g, w_out, norm2_g, w_up, w_down, final_norm_g):
    lb_all = jnp.cumsum(jax.nn.softmax(hg_lower_bounds.astype(F32), axis=0), axis=0)
    lb_all = lb_all - lb_all[0:1]
    weights = dict(
        norm1_g=norm1_g, w_in=w_in.astype(BF16), ret_norm_g=ret_norm_g, pool_w=pool_w.astype(BF16),
        pool_scale=pool_scale, hg_norm_g=hg_norm_g, w_out=w_out.astype(BF16), norm2_g=norm2_g,
        w_up=w_up.astype(BF16), w_down=w_down.astype(BF16), final_norm_g=final_norm_g)
    past = cache_sb_k.shape[2]
    y_p, ret_p, sbk_p, sbv_p, pool_p, hg_p = _run_trunk(x_prompt, 0, None, weights, lb_all)
    y_s, ret_s, sbk_s, sbv_s, pool_s, hg_s = _run_trunk(
        x_sample, past, (state_ret, cache_sb_k, cache_sb_v, state_pool, state_hgrn), weights, lb_all)
    return (y_p, y_s, ret_p, ret_s, sbk_p, sbv_p, sbk_s, sbv_s, pool_p, pool_s, hg_p, hg_s)
```

```python
import functools
import math

import numpy as np
import jax
import jax.numpy as jnp
from jax import lax
from jax.experimental import pallas as pl
from jax.experimental.pallas import tpu as pltpu

F32 = jnp.float32
BF16 = jnp.bfloat16

HEAD_DIM = 128
N_HEADS = 4
GROUP_W = N_HEADS * HEAD_DIM
N_SLOTS = 12
POOL_WINDOWS = (2, 4, 8, 16)
POOL_PAD = max(POOL_WINDOWS) - 1
ROPE_BASE = 10000.0
EPS = 1e-6
SB_SCALE = HEAD_DIM ** -0.5
SB_DEAD_LOG = -104.0

VMEM_LIMIT = 56 * 1024 * 1024
ROW_GROUP = 256


def _cparams(*sem):
    return pltpu.CompilerParams(dimension_semantics=sem, vmem_limit_bytes=VMEM_LIMIT)


def _dot(a, b):
    return jnp.dot(a, b, preferred_element_type=F32)


def _dot_nt(a, b):
    return lax.dot_general(a, b, (((1,), (1,)), ((), ())), preferred_element_type=F32)


def _sigmoid(x):
    return 1.0 / (1.0 + jnp.exp(-x))


def _log_sigmoid(x):
    return jnp.minimum(x, 0.0) - jnp.log(1.0 + jnp.exp(-jnp.abs(x)))


def _rms(x, g):
    ms = jnp.mean(x * x, axis=-1, keepdims=True)
    return x * lax.rsqrt(ms + EPS) * g


def _hcols(h):
    return slice(h * HEAD_DIM, (h + 1) * HEAD_DIM)


def _norm_matmul_kernel(x_ref, g_ref, w_ref, o_ref):
    h = _rms(x_ref[...], g_ref[...]).astype(BF16)
    o_ref[...] = _dot(h, w_ref[...])


def norm_matmul(x, g, w, *, tm=256):
    t, d = x.shape
    n = w.shape[1]
    tm = min(tm, t)
    return pl.pallas_call(
        _norm_matmul_kernel,
        grid=(t // tm,),
        in_specs=[
            pl.BlockSpec((tm, d), lambda i: (i, 0)),
            pl.BlockSpec((1, d), lambda i: (0, 0)),
            pl.BlockSpec((d, n), lambda i: (0, 0)),
        ],
        out_specs=pl.BlockSpec((tm, n), lambda i: (i, 0)),
        out_shape=jax.ShapeDtypeStruct((t, n), F32),
        compiler_params=_cparams("parallel"),
        name="norm_in_proj",
    )(x, g.reshape(1, d), w)


def _out_proj_kernel(x_ref, m0_ref, m1_ref, m2_ref, m3_ref, w_ref, o_ref):
    acc = x_ref[...]
    for gi, m_ref in enumerate((m0_ref, m1_ref, m2_ref, m3_ref)):
        acc = acc + _dot(m_ref[...], w_ref[gi * GROUP_W:(gi + 1) * GROUP_W, :])
    o_ref[...] = acc


def out_proj(x, mixes, w, layer, *, tm=512):
    t, d = x.shape
    tm = min(tm, t)
    mix_spec = pl.BlockSpec((tm, GROUP_W), lambda i: (i, 0))
    return pl.pallas_call(
        _out_proj_kernel,
        grid=(t // tm,),
        in_specs=[pl.BlockSpec((tm, d), lambda i: (i, 0)), mix_spec, mix_spec, mix_spec, mix_spec,
                  pl.BlockSpec((None,) + w.shape[1:], lambda i: (layer, 0, 0), pipeline_mode=pl.Buffered(1))],
        out_specs=pl.BlockSpec((tm, d), lambda i: (i, 0)),
        out_shape=jax.ShapeDtypeStruct((t, d), F32),
        compiler_params=_cparams("parallel"),
        name="out_proj",
    )(x, *mixes, w)


def _mlp_kernel(x_ref, g_ref, wu_ref, wd_ref, *rest, final, n_cast):
    rest = list(rest)
    fg_ref = rest.pop(0) if final else None
    cast_in = [rest.pop(0) for _ in range(n_cast)]
    o_ref = rest.pop(0)
    cast_out = [rest.pop(0) for _ in range(n_cast)]
    (h_ref,) = rest
    f = pl.program_id(1)

    @pl.when(f == 0)
    def _():
        x = x_ref[...]
        h_ref[...] = _rms(x, g_ref[...]).astype(BF16)
        o_ref[...] = x

    u = _dot(h_ref[...], wu_ref[...])
    a = jnp.square(jnp.maximum(u, 0.0)).astype(BF16)
    o_ref[...] += _dot(a, wd_ref[...])

    for src_ref, dst_ref in zip(cast_in, cast_out):
        dst_ref[...] = src_ref[...].astype(BF16)

    if final:
        @pl.when(f == pl.num_programs(1) - 1)
        def _():
            o_ref[...] = _rms(o_ref[...], fg_ref[...])


def mlp(x, g, w_up, w_down, final_g=None, cast_next=None, *, tm=1024, tf=512):
    t, d = x.shape
    dff = w_up.shape[1]
    tm = min(tm, t)
    nf = dff // tf
    steps = (t // tm) * nf
    final = final_g is not None
    vec_spec = pl.BlockSpec((1, d), lambda i, f: (0, 0))
    in_specs = [pl.BlockSpec((tm, d), lambda i, f: (i, 0)), vec_spec,
                pl.BlockSpec((d, tf), lambda i, f: (0, f)), pl.BlockSpec((tf, d), lambda i, f: (f, 0))]
    args = [x, g.reshape(1, d), w_up, w_down]
    if final:
        in_specs.append(vec_spec)
        args.append(final_g.reshape(1, d))
    out_specs = [pl.BlockSpec((tm, d), lambda i, f: (i, 0))]
    out_shape = [jax.ShapeDtypeStruct((t, d), F32)]
    n_cast = 0
    if cast_next is not None:
        layer, stacked = cast_next
        n_cast = len(stacked)
        for w in stacked:
            _, r, c = w.shape
            slab = r // steps
            assert slab * steps == r and slab % 16 == 0, (w.shape, steps)
            in_specs.append(pl.BlockSpec((None, slab, c), lambda i, f: (layer, i * nf + f, 0)))
            args.append(w)
            out_specs.append(pl.BlockSpec((slab, c), lambda i, f: (i * nf + f, 0)))
            out_shape.append(jax.ShapeDtypeStruct((r, c), BF16))
    out = pl.pallas_call(
        functools.partial(_mlp_kernel, final=final, n_cast=n_cast),
        grid=(t // tm, nf),
        in_specs=in_specs,
        out_specs=out_specs,
        out_shape=out_shape,
        scratch_shapes=[pltpu.VMEM((tm, d), BF16)],
        compiler_params=_cparams("parallel", "arbitrary"),
        name="mlp",
    )(*args)
    return (out[0], out[1:]) if n_cast else out[0]


def _zslot(slot):
    return lambda g: (g, slot)


def _head_norm_gate(o, gn, gate):
    return (_rms(o, gn) * gate).astype(BF16)


def _state_specs(state, layer, nseq):
    blk = (nseq, N_HEADS, HEAD_DIM, HEAD_DIM)
    in_spec = pl.BlockSpec((None,) + blk, lambda g: (layer, g, 0, 0, 0))
    out_spec = pl.BlockSpec(blk, lambda g: (g, 0, 0, 0))
    return in_spec, out_spec, jax.ShapeDtypeStruct(state.shape[1:], F32)


_FRESH_STATE_SHAPE = jax.ShapeDtypeStruct((N_HEADS, HEAD_DIM, HEAD_DIM), F32)
_FRESH_STATE_SPEC = pl.BlockSpec((N_HEADS, HEAD_DIM, HEAD_DIM), lambda g: (0, 0, 0))


def _group_shape(t, batch, seq, fresh):
    rows = min(ROW_GROUP, t)
    if fresh:
        assert batch == 1 and t % rows == 0
        return rows, rows, 1
    assert rows % seq == 0 and batch % (rows // seq) == 0
    return rows, seq, rows // seq


def _retention_kernel(lg_ref, q_ref, k_ref, v_ref, g_ref, cos_ref, sin_ref, gn_ref, *rest,
                      chunk, nseq, carry):
    if carry:
        o_ref, sout_ref, dmask_ref, qdec_ref, kdec_ref, state_ref = rest
        s0_ref = None
    else:
        s0_ref, o_ref, sout_ref, dmask_ref, qdec_ref, kdec_ref = rest
        state_ref = None
    rows = chunk * nseq
    g = pl.program_id(0)

    @pl.when(g == 0)
    def _():
        row = lax.broadcasted_iota(jnp.int32, (rows, rows), 0)
        col = lax.broadcasted_iota(jnp.int32, (rows, rows), 1)
        diff = row - col
        ok = jnp.where((row ^ col) < chunk, diff, -1) >= 0
        dist = jnp.maximum(diff, 0).astype(F32)
        pos = (lax.broadcasted_iota(jnp.int32, (rows, HEAD_DIM), 0) & (chunk - 1)).astype(F32)
        for h in range(N_HEADS):
            lg = lg_ref[h]
            dmask_ref[h] = jnp.where(ok, jnp.exp(dist * lg), 0.0)
            qdec_ref[h] = jnp.exp((pos + 1.0) * lg)
            kdec_ref[h] = jnp.exp((chunk - 1.0 - pos) * lg)
        if carry:
            state_ref[...] = jnp.zeros_like(state_ref)

    cos = cos_ref[...]
    sin = sin_ref[...]

    def rope(x):
        return x * cos + pltpu.roll(x, HEAD_DIM // 2, 1) * sin

    for h in range(N_HEADS):
        hc = _hcols(h)
        q = rope(q_ref[:, hc])
        k = rope(k_ref[:, hc]) * SB_SCALE
        vb = v_ref[:, hc].astype(BF16)
        scores = _dot_nt(q.astype(BF16), k.astype(BF16)) * dmask_ref[h]
        o_intra = _dot(scores.astype(BF16), vb)
        qd = (q * qdec_ref[h]).astype(BF16)
        kd = k * kdec_ref[h]
        cdec = qdec_ref[h, chunk - 1:chunk, :]
        outs = []
        for b in range(nseq):
            sl = slice(b * chunk, (b + 1) * chunk)
            s_old = state_ref[h] if carry else s0_ref[b, h]
            outs.append(o_intra[sl] + _dot(qd[sl], s_old.astype(BF16)))
            s_new = cdec * s_old + _dot(kd[sl].T.astype(BF16), vb[sl])
            if carry:
                state_ref[h] = s_new
            else:
                sout_ref[b, h] = s_new
        o = outs[0] if nseq == 1 else jnp.concatenate(outs, axis=0)
        gate = g_ref[:, hc]
        o_ref[:, hc] = _head_norm_gate(o, gn_ref[:, hc], gate * _sigmoid(gate))

    if carry:
        @pl.when(g == pl.num_programs(0) - 1)
        def _():
            sout_ref[...] = state_ref[...]


def retention(z, cos, sin, gn, state, layer, *, batch, seq):
    t = z.shape[0]
    carry = state is None
    rows, chunk, nseq = _group_shape(t, batch, seq, carry)
    lg = jnp.asarray(np.log(1.0 - 2.0 ** (-5.0 - np.arange(N_HEADS))), F32)
    blk = (rows, GROUP_W)
    tab_blk = (rows, HEAD_DIM)
    tab_idx = (lambda g: (g, 0)) if carry else (lambda g: (0, 0))
    in_specs = [
        pl.BlockSpec(memory_space=pltpu.SMEM),
        pl.BlockSpec(blk, _zslot(0)), pl.BlockSpec(blk, _zslot(1)),
        pl.BlockSpec(blk, _zslot(2)), pl.BlockSpec(blk, _zslot(3)),
        pl.BlockSpec(tab_blk, tab_idx), pl.BlockSpec(tab_blk, tab_idx),
        pl.BlockSpec((1, GROUP_W), lambda g: (0, 0)),
    ]
    args = [lg, z, z, z, z, cos, sin, gn.reshape(1, GROUP_W)]
    scratch = [pltpu.VMEM((N_HEADS, rows, rows), F32), pltpu.VMEM((N_HEADS,) + tab_blk, F32),
               pltpu.VMEM((N_HEADS,) + tab_blk, F32)]
    if carry:
        sout_shape, sout_spec = _FRESH_STATE_SHAPE, _FRESH_STATE_SPEC
        scratch.append(pltpu.VMEM((N_HEADS, HEAD_DIM, HEAD_DIM), F32))
    else:
        st_spec, sout_spec, sout_shape = _state_specs(state, layer, nseq)
        in_specs.append(st_spec)
        args.append(state)
    o, sout = pl.pallas_call(
        functools.partial(_retention_kernel, chunk=chunk, nseq=nseq, carry=carry),
        grid=(t // rows,),
        in_specs=in_specs,
        out_specs=[pl.BlockSpec(blk, lambda g: (g, 0)), sout_spec],
        out_shape=[jax.ShapeDtypeStruct((t, GROUP_W), BF16), sout_shape],
        scratch_shapes=scratch,
        compiler_params=_cparams("arbitrary"),
        name="retention",
    )(*args)
    return o, sout


def _strict_upper_ones_twice(n):
    row = lax.broadcasted_iota(jnp.int32, (2 * n, n), 0) & (n - 1)
    col = lax.broadcasted_iota(jnp.int32, (2 * n, n), 1)
    return jnp.where(row > col, 1.0, 0.0).astype(BF16)


def _sb_block(qb, kblk, vblk, carry, acc, *, diag):
    bq, bk = qb.shape[0], kblk.shape[0]
    nz = _dot_nt(qb, kblk.astype(BF16)) * (-SB_SCALE)
    lf = _log_sigmoid(nz)
    if diag:
        row = lax.broadcasted_iota(jnp.int32, (bq, bk), 0)
        col = lax.broadcasted_iota(jnp.int32, (bq, bk), 1)
        valid = col < row
        lf = jnp.where(valid, lf, 0.0)
    hi = lf.astype(BF16)
    lo = (lf - hi.astype(F32)).astype(BF16)
    later_in = _dot(jnp.concatenate([hi, lo], axis=1), _strict_upper_ones_twice(bk))
    a = jnp.exp(lf - nz + later_in + carry)
    if diag:
        a = jnp.where(valid, a, 0.0)
    acc = acc + _dot(a.astype(BF16), vblk.astype(BF16))
    carry = carry + later_in[:, 0:1] + lf[:, 0:1]
    return carry, acc


def _sb_kernel(q_ref, kd_ref, vd_ref, kp_ref, vp_ref, kfar_ref, vfar_ref, o_ref, kbuf_ref, vbuf_ref, sem_ref,
               *, bq, bk, fresh, layer, past):
    g = pl.program_id(0)
    qs = [q_ref[:, _hcols(h)].astype(BF16) for h in range(N_HEADS)]

    if fresh:
        def head_rows(ref, h):
            return ref[:, _hcols(h)]
        n_blocks = g
    else:
        def head_rows(ref, h):
            return ref[pl.ds(h, bk, stride=N_HEADS), :]
        n_blocks = past // bk

    def sweep(carries, accs, k_ref, v_ref, get, diag):
        out = [_sb_block(qs[h], get(k_ref, h), get(v_ref, h), carries[h], accs[h], diag=diag)
               for h in range(N_HEADS)]
        return tuple(c for c, _ in out), tuple(a for _, a in out)

    def alive(carries):
        m = functools.reduce(jnp.maximum, carries)
        return jnp.max(m) > SB_DEAD_LOG

    carries = tuple(jnp.zeros((bq, 1), F32) for _ in range(N_HEADS))
    accs = tuple(jnp.zeros((bq, HEAD_DIM), F32) for _ in range(N_HEADS))
    carries, accs = sweep(carries, accs, kd_ref, vd_ref, lambda ref, h: ref[:, _hcols(h)], True)

    near = sweep(carries, accs, kp_ref, vp_ref, head_rows, False)
    if fresh:
        has_past = g >= 1
        carries, accs = jax.tree.map(lambda new, old: jnp.where(has_past, new, old), near, (carries, accs))
    else:
        carries, accs = near

    def cond(st):
        c, carries, _ = st
        return jnp.logical_and(c < n_blocks, alive(carries))

    def body(st):
        c, carries, accs = st
        if fresh:
            start = pl.multiple_of((g - 1 - c) * bk, bk)
            k_src = kfar_ref.at[pl.ds(start, bk), pl.ds(5 * GROUP_W, GROUP_W)]
            v_src = vfar_ref.at[pl.ds(start, bk), pl.ds(6 * GROUP_W, GROUP_W)]
        else:
            start = pl.multiple_of((past - (c + 1) * bk) * N_HEADS, bk * N_HEADS)
            k_src = kfar_ref.at[layer, g, pl.ds(start, bk * N_HEADS), :]
            v_src = vfar_ref.at[layer, g, pl.ds(start, bk * N_HEADS), :]
        k_copy = pltpu.make_async_copy(k_src, kbuf_ref, sem_ref.at[0])
        v_copy = pltpu.make_async_copy(v_src, vbuf_ref, sem_ref.at[1])
        k_copy.start()
        v_copy.start()
        k_copy.wait()
        v_copy.wait()
        carries, accs = sweep(carries, accs, kbuf_ref, vbuf_ref, head_rows, False)
        return c + 1, carries, accs

    _, _, accs = lax.while_loop(cond, body, (jnp.int32(1), carries, accs))
    for h in range(N_HEADS):
        o_ref[:, _hcols(h)] = accs[h].astype(BF16)


def stick_breaking(z, k_past, v_past, layer, *, batch, seq):
    t = z.shape[0]
    fresh = k_past is None
    if fresh:
        assert batch == 1
        bq = bk = min(ROW_GROUP, t)
        past = None
        prev_idx = lambda slot: (lambda g: (jnp.maximum(g - 1, 0), slot))
        past_specs = [pl.BlockSpec((bk, GROUP_W), prev_idx(5)), pl.BlockSpec((bk, GROUP_W), prev_idx(6))]
        k_far, v_far = z, z
        buf_shape = (bk, GROUP_W)
    else:
        bq = seq
        past = k_past.shape[2] // N_HEADS
        bk = min(ROW_GROUP, past)
        assert past % bk == 0
        last = past // bk - 1
        tail = pl.BlockSpec((None, None, bk * N_HEADS, HEAD_DIM), lambda g: (layer, g, last, 0))
        past_specs = [tail, tail]
        k_far, v_far = k_past, v_past
        buf_shape = (bk * N_HEADS, HEAD_DIM)
    blk = (bq, GROUP_W)
    any_spec = pl.BlockSpec(memory_space=pl.ANY)
    return pl.pallas_call(
        functools.partial(_sb_kernel, bq=bq, bk=bk, fresh=fresh, layer=layer, past=past),
        grid=(t // bq,),
        in_specs=[pl.BlockSpec(blk, _zslot(4)), pl.BlockSpec(blk, _zslot(5)), pl.BlockSpec(blk, _zslot(6)),
                  *past_specs, any_spec, any_spec],
        out_specs=pl.BlockSpec(blk, lambda g: (g, 0)),
        out_shape=jax.ShapeDtypeStruct((t, GROUP_W), BF16),
        scratch_shapes=[pltpu.VMEM(buf_shape, F32), pltpu.VMEM(buf_shape, F32), pltpu.SemaphoreType.DMA((2,))],
        compiler_params=_cparams("arbitrary"),
        name="stick_breaking",
    )(z, z, z, k_far, v_far, k_far, v_far)


def _emit_kv_kernel(*refs, depth, rows):
    k_refs, v_refs = refs[:depth], refs[depth:2 * depth]
    ko_ref, vo_ref = refs[2 * depth:]
    for l in range(depth):
        for h in range(N_HEADS):
            ko_ref[l, pl.ds(h, rows, stride=N_HEADS), :] = k_refs[l][:, _hcols(h)]
            vo_ref[l, pl.ds(h, rows, stride=N_HEADS), :] = v_refs[l][:, _hcols(h)]


def emit_kv(zs, *, rows=ROW_GROUP):
    depth = len(zs)
    t = zs[0].shape[0]
    rows = min(rows, t)
    out_shape = jax.ShapeDtypeStruct((depth, t * N_HEADS, HEAD_DIM), F32)
    out_spec = pl.BlockSpec((depth, rows * N_HEADS, HEAD_DIM), lambda g: (0, g, 0))
    blk = (rows, GROUP_W)
    return pl.pallas_call(
        functools.partial(_emit_kv_kernel, depth=depth, rows=rows),
        grid=(t // rows,),
        in_specs=[pl.BlockSpec(blk, _zslot(5))] * depth + [pl.BlockSpec(blk, _zslot(6))] * depth,
        out_specs=[out_spec, out_spec],
        out_shape=[out_shape, out_shape],
        compiler_params=_cparams("parallel"),
        name="emit_kv",
    )(*zs, *zs)


POOL_HEAD = 16


def _pool_kernel(u_ref, *rest, rows, pos0, carry):
    if carry:
        w_ref, sc_ref, o_ref, ext_ref = rest
        s0_ref = None
    else:
        s0_ref, w_ref, sc_ref, o_ref, ext_ref = rest
    g = pl.program_id(0)
    if carry:
        @pl.when(g == 0)
        def _():
            ext_ref[0:POOL_HEAD, :] = jnp.zeros((POOL_HEAD, GROUP_W), F32)
        base = g * rows + pos0
    else:
        ext_ref[0:POOL_HEAD, :] = s0_ref[...]
        base = pos0
    ext_ref[POOL_HEAD:POOL_HEAD + rows, :] = u_ref[...]
    pos1 = lax.broadcasted_iota(jnp.int32, (rows, HEAD_DIM), 0) + (base + 1)
    for gi, w in enumerate(POOL_WINDOWS):
        cols = _hcols(gi)
        tok = ext_ref[POOL_HEAD:POOL_HEAD + rows, cols]
        wsum = tok
        for i in range(1, w):
            wsum = wsum + ext_ref[POOL_HEAD - i:POOL_HEAD - i + rows, cols]
        cnt = jnp.minimum(pos1, w).astype(F32)
        p = wsum / cnt - tok
        y = _dot(p.astype(BF16), w_ref[gi]) * sc_ref[:, cols]
        o_ref[:, cols] = y.astype(BF16)
    if carry:
        ext_ref[0:POOL_HEAD, :] = ext_ref[rows:rows + POOL_HEAD, :]


def multi_pool(z, state, pool_w, pool_scale, layer, *, batch, seq, pos0):
    t = z.shape[0]
    carry = state is None
    rows = min(ROW_GROUP, t) if carry else seq
    in_specs = [pl.BlockSpec((rows, GROUP_W), _zslot(7))]
    args = [z]
    if not carry:
        in_specs.append(pl.BlockSpec((None, None, POOL_HEAD, GROUP_W), lambda g: (layer, g, 0, 0)))
        args.append(state)
    in_specs += [pl.BlockSpec((None,) + pool_w.shape[1:], lambda g: (layer, 0, 0, 0)),
                 pl.BlockSpec((1, GROUP_W), lambda g: (0, 0))]
    args += [pool_w, pool_scale.reshape(1, GROUP_W)]
    return pl.pallas_call(
        functools.partial(_pool_kernel, rows=rows, pos0=pos0, carry=carry),
        grid=(t // rows,),
        in_specs=in_specs,
        out_specs=pl.BlockSpec((rows, GROUP_W), lambda g: (g, 0)),
        out_shape=jax.ShapeDtypeStruct((t, GROUP_W), BF16),
        scratch_shapes=[pltpu.VMEM((POOL_HEAD + rows, GROUP_W), F32)],
        compiler_params=_cparams("arbitrary"),
        name="multi_pool",
    )(*args)


def _hgrn_tables(rows, chunk):
    nlev = int(math.log2(chunk))
    l = np.arange(rows)[:, None]
    m = np.arange(rows)[None, :]
    same = (l // chunk) == (m // chunk)
    x = np.bitwise_xor(l, m)
    hb = np.floor(np.log2(np.maximum(x, 1))).astype(np.int64)
    level = nlev - 1 - hb
    lv = np.where(same & (l > m), level, np.where(l == m, nlev, -1)).astype(np.float32)
    tri = (same & (l >= m)).astype(np.float32)
    return jnp.asarray(lv, BF16), jnp.asarray(np.tile(tri, (1, 3)), BF16)


def _split3(x):
    h1 = x.astype(BF16)
    r1 = x - h1.astype(F32)
    h2 = r1.astype(BF16)
    h3 = (r1 - h2.astype(F32)).astype(BF16)
    return h1, h2, h3


def _hgrn_head(gq, fl, vi, gate, lb, gn, lv, tri, gcum_ref, state_in, chunk, nseq):
    rows = chunk * nseq
    nlev = int(math.log2(chunk))
    q = gq * _sigmoid(gq)
    log_lb, log1m_lb, one_m_lb = lb[0:1, :], lb[1:2, :], lb[2:3, :]
    t = jnp.exp(-jnp.abs(fl))
    r = 1.0 / (1.0 + t)
    sig_neg = jnp.where(fl >= 0.0, t * r, r)
    b = log1m_lb + (jnp.minimum(fl, 0.0) - jnp.log(1.0 + t))
    log_f = jnp.maximum(log_lb, b) + jnp.log(1.0 + jnp.exp(-jnp.abs(log_lb - b)))
    kk = one_m_lb * sig_neg
    vb = vi.astype(BF16)

    h1, h2, h3 = _split3(log_f)
    gcum = _dot(tri, jnp.concatenate([h1, h2, h3], axis=0))
    gcum_ref[...] = gcum

    ridx = lax.broadcasted_iota(jnp.int32, (rows, HEAD_DIM), 0)
    qb = q.astype(BF16)
    kb = kk.astype(BF16)
    zero_scores = jnp.zeros((rows, rows), BF16)
    scores = jnp.where(lv == nlev, _dot_nt(qb, kb).astype(BF16), zero_scores)

    def add_level(scores, level, e):
        w = jnp.exp(e).astype(BF16)
        s = _dot_nt(qb * w, kb * w)
        return jnp.where(lv == level, s.astype(BF16), scores)

    for level in range(nlev):
        blk = chunk >> level
        if blk < 16:
            break
        half = blk // 2
        pieces = [jnp.broadcast_to(gcum_ref[r0 + half - 1:r0 + half, :], (blk, HEAD_DIM))
                  for r0 in range(0, rows, blk)]
        gmid = pieces[0] if len(pieces) == 1 else jnp.concatenate(pieces, axis=0)
        scores = add_level(scores, level, -jnp.abs(gcum - gmid))

    f1 = pltpu.roll(log_f, 1, 0)
    f2 = pltpu.roll(log_f, 2, 0)
    f3 = pltpu.roll(log_f, 3, 0)
    b1 = pltpu.roll(log_f, rows - 1, 0)
    b2 = pltpu.roll(log_f, rows - 2, 0)
    b3 = pltpu.roll(log_f, rows - 3, 0)
    a1 = log_f
    a2 = a1 + f1
    a3 = a2 + f2
    a4 = a3 + f3
    c1 = b1
    c2 = c1 + b2
    c3 = c2 + b3
    zero = jnp.zeros_like(log_f)

    def pick(idx, table):
        out = table[-1]
        for j in range(len(table) - 2, -1, -1):
            out = jnp.where(idx == j, table[j], out)
        return out

    small = {8: pick(ridx & 7, [c3, c2, c1, zero, a1, a2, a3, a4]),
             4: pick(ridx & 3, [c1, zero, a1, a2]),
             2: pick(ridx & 1, [zero, a1])}
    for blk in (8, 4, 2):
        if blk <= chunk:
            scores = add_level(scores, nlev - int(math.log2(blk)), small[blk])

    o_intra = _dot(scores.astype(BF16), vb)
    qg = (q * jnp.exp(gcum)).astype(BF16)
    outs, states = [], []
    for bi in range(nseq):
        sl = slice(bi * chunk, (bi + 1) * chunk)
        glast = gcum_ref[(bi + 1) * chunk - 1:(bi + 1) * chunk, :]
        st_old = state_in(bi)
        outs.append(o_intra[sl] + _dot_nt(qg[sl], st_old.astype(BF16)))
        kdec = (kk[sl] * jnp.exp(glast - gcum[sl])).astype(BF16)
        states.append(st_old * jnp.exp(glast) + _dot(vi[sl].T.astype(BF16), kdec))
    o = outs[0] if nseq == 1 else jnp.concatenate(outs, axis=0)
    return _head_norm_gate(o, gn, _sigmoid(gate)), states


def _hgrn_kernel(q_ref, f_ref, i_ref, g_ref, lb_ref, gn_ref, lv_ref, tri_ref, *rest, chunk, nseq, carry):
    if carry:
        o_ref, sout_ref, gcum_ref, state_ref = rest
        s0_ref = None
    else:
        s0_ref, o_ref, sout_ref, gcum_ref = rest
        state_ref = None
    g = pl.program_id(0)

    if carry:
        @pl.when(g == 0)
        def _():
            state_ref[...] = jnp.zeros_like(state_ref)

    lv = lv_ref[...]
    tri = tri_ref[...]
    for h in range(N_HEADS):
        hc = _hcols(h)
        if carry:
            state_in = lambda bi, h=h: state_ref[h]
        else:
            state_in = lambda bi, h=h: s0_ref[bi, h].T
        o, states = _hgrn_head(q_ref[:, hc], f_ref[:, hc], i_ref[:, hc], g_ref[:, hc], lb_ref[:, hc],
                               gn_ref[:, hc], lv, tri, gcum_ref.at[h], state_in, chunk, nseq)
        o_ref[:, hc] = o
        for bi, st in enumerate(states):
            if carry:
                state_ref[h] = st
            else:
                sout_ref[bi, h] = st.T

    if carry:
        @pl.when(g == pl.num_programs(0) - 1)
        def _():
            for h in range(N_HEADS):
                sout_ref[h] = state_ref[h].T


def hgrn2(z, lb, gn, state, layer, *, batch, seq):
    t = z.shape[0]
    carry = state is None
    rows, chunk, nseq = _group_shape(t, batch, seq, carry)
    lv, tri = _hgrn_tables(rows, chunk)
    lb_tab = jnp.stack([jnp.log(lb), jnp.log1p(-lb), 1.0 - lb], axis=0)
    blk = (rows, GROUP_W)
    const2 = lambda g: (0, 0)
    in_specs = [
        pl.BlockSpec(blk, _zslot(8)), pl.BlockSpec(blk, _zslot(9)),
        pl.BlockSpec(blk, _zslot(10)), pl.BlockSpec(blk, _zslot(11)),
        pl.BlockSpec((3, GROUP_W), const2), pl.BlockSpec((1, GROUP_W), const2),
        pl.BlockSpec((rows, rows), const2), pl.BlockSpec((rows, 3 * rows), const2),
    ]
    args = [z, z, z, z, lb_tab, gn.reshape(1, GROUP_W), lv, tri]
    scratch = [pltpu.VMEM((N_HEADS, rows, HEAD_DIM), F32)]
    if carry:
        sout_shape, sout_spec = _FRESH_STATE_SHAPE, _FRESH_STATE_SPEC
        scratch.append(pltpu.VMEM((N_HEADS, HEAD_DIM, HEAD_DIM), F32))
    else:
        st_spec, sout_spec, sout_shape = _state_specs(state, layer, nseq)
        in_specs.append(st_spec)
        args.append(state)
    o, sout = pl.pallas_call(
        functools.partial(_hgrn_kernel, chunk=chunk, nseq=nseq, carry=carry),
        grid=(t // rows,),
        in_specs=in_specs,
        out_specs=[pl.BlockSpec(blk, lambda g: (g, 0)), sout_spec],
        out_shape=[jax.ShapeDtypeStruct((t, GROUP_W), BF16), sout_shape],
        scratch_shapes=scratch,
        compiler_params=_cparams("arbitrary"),
        name="hgrn2",
    )(*args)
    return o, sout


def _rope_tables(pos):
    half = HEAD_DIM // 2
    inv = ROPE_BASE ** (-jnp.arange(half, dtype=F32) / half)
    ang = pos.astype(F32)[:, None] * inv[None, :]
    cos = jnp.cos(ang)
    sin = jnp.sin(ang)
    return jnp.concatenate([cos, cos], axis=-1), jnp.concatenate([-sin, sin], axis=-1)


class _Group:
    def __init__(self, x, pos0, states, depth):
        self.batch, self.seq, self.d = x.shape
        self.pos0 = pos0
        t = self.batch * self.seq
        self.x2 = x.reshape(t, self.d)
        rows = min(ROW_GROUP, t)
        self.cos, self.sin = _rope_tables(pos0 + jnp.arange(self.seq, dtype=jnp.int32))
        if states is None:
            self.ret_s = self.k_past = self.v_past = self.pool_s = self.hg_s = None
        else:
            reps = rows // self.seq
            self.cos, self.sin = jnp.tile(self.cos, (reps, 1)), jnp.tile(self.sin, (reps, 1))
            self.ret_s, self.hg_s = states[0], states[4]
            past = states[1].shape[2]
            self.k_past = states[1].reshape(depth, self.batch, past * N_HEADS, HEAD_DIM)
            self.v_past = states[2].reshape(depth, self.batch, past * N_HEADS, HEAD_DIM)
            self.pool_s = jnp.pad(states[3], ((0, 0), (0, 0), (POOL_HEAD - POOL_PAD, 0), (0, 0)))
        self.zs, self.rets, self.pools, self.hgs = [], [], [], []

    def layer(self, l, p, lb, wb, final_g, cast_next):
        bs = dict(batch=self.batch, seq=self.seq)
        z = norm_matmul(self.x2, p["norm1_g"][l], wb[0])
        ro, ret_new = retention(z, self.cos, self.sin, p["ret_norm_g"][l], self.ret_s, l, **bs)
        so = stick_breaking(z, self.k_past, self.v_past, l, **bs)
        po = multi_pool(z, self.pool_s, p["pool_w"], p["pool_scale"][l], l, pos0=self.pos0, **bs)
        ho, hg_new = hgrn2(z, lb, p["hg_norm_g"][l], self.hg_s, l, **bs)
        x1 = out_proj(self.x2, (ro, so, po, ho), p["w_out"], l)
        out = mlp(x1, p["norm2_g"][l], wb[1], wb[2], final_g, cast_next)
        self.x2, casts = out if cast_next is not None else (out, None)
        state_shape = (self.batch, N_HEADS, HEAD_DIM, HEAD_DIM)
        self.zs.append(z)
        self.rets.append(ret_new.reshape(state_shape))
        self.hgs.append(hg_new.reshape(state_shape))
        z3 = z.reshape(self.batch, self.seq, N_SLOTS * GROUP_W)
        self.pools.append(z3[:, self.seq - POOL_PAD:, 7 * GROUP_W:8 * GROUP_W])
        return casts

    def outputs(self):
        ks, vs = emit_kv(self.zs)
        kv_shape = (len(self.zs), self.batch, self.seq, N_HEADS, HEAD_DIM)
        y = self.x2.reshape(self.batch, self.seq, self.d)
        return (y, jnp.stack(self.rets), ks.reshape(kv_shape), vs.reshape(kv_shape),
                jnp.stack(self.pools), jnp.stack(self.hgs))


def kernel(x_prompt, x_sample, state_ret, cache_sb_k, cache_sb_v, state_pool, state_hgrn, norm1_g, w_in, ret_norm_g, pool_w, pool_scale, hg_lower_bounds, hg_norm_g, w_out, norm2_g, w_up, w_down, final_norm_g):
    lb_all = jnp.cumsum(jax.nn.softmax(hg_lower_bounds.astype(F32), axis=0), axis=0)
    lb_all = lb_all - lb_all[0:1]
    depth = w_in.shape[0]
    params = dict(norm1_g=norm1_g, ret_norm_g=ret_norm_g, pool_w=pool_w.astype(BF16), pool_scale=pool_scale,
                  hg_norm_g=hg_norm_g, w_out=w_out.astype(BF16), norm2_g=norm2_g)
    big = [w_in, w_up, w_down]
    prompt = _Group(x_prompt, 0, None, depth)
    streams = _Group(x_sample, cache_sb_k.shape[2],
                     (state_ret, cache_sb_k, cache_sb_v, state_pool, state_hgrn), depth)
    wb = tuple(w[0].astype(BF16) for w in big)
    for l in range(depth):
        final_g = final_norm_g if l == depth - 1 else None
        cast_next = (l + 1, big) if l + 1 < depth else None
        wb_next = prompt.layer(l, params, lb_all[l], wb, final_g, cast_next)
        streams.layer(l, params, lb_all[l], wb, final_g, None)
        wb = wb_next
    y_p, ret_p, sbk_p, sbv_p, pool_p, hg_p = prompt.outputs()
    y_s, ret_s, sbk_s, sbv_s, pool_s, hg_s = streams.outputs()
    return (y_p, y_s, ret_p, ret_s, sbk_p, sbv_p, sbk_s, sbv_s, pool_p, pool_s, hg_p, hg_s)
```

```python
import functools
import math

import numpy as np
import jax
import jax.numpy as jnp
from jax import lax
from jax.experimental import pallas as pl
from jax.experimental.pallas import tpu as pltpu

F32 = jnp.float32
BF16 = jnp.bfloat16

HEAD_DIM = 128
N_HEADS = 4
GROUP_W = N_HEADS * HEAD_DIM
N_SLOTS = 12
SB_SLOT0, N_SB_SLOTS = 4, 3
POOL_WINDOWS = (2, 4, 8, 16)
POOL_PAD = max(POOL_WINDOWS) - 1
POOL_HEAD = 16
ROPE_BASE = 10000.0
EPS = 1e-6
SB_SCALE = HEAD_DIM ** -0.5
SB_DEAD_LOG = -104.0

VMEM_LIMIT = 56 * 1024 * 1024
ROW_GROUP = 256
HGRN_CHUNK = 64


def _cparams(*sem):
    return pltpu.CompilerParams(dimension_semantics=sem, vmem_limit_bytes=VMEM_LIMIT)


def _dot(a, b):
    return jnp.dot(a, b, preferred_element_type=F32)


def _dot_nt(a, b):
    return lax.dot_general(a, b, (((1,), (1,)), ((), ())), preferred_element_type=F32)


def _sigmoid(x):
    return 1.0 / (1.0 + jnp.exp(-x))


def _log_sigmoid(x):
    return jnp.minimum(x, 0.0) - jnp.log(1.0 + jnp.exp(-jnp.abs(x)))


def _rms(x, g):
    ms = jnp.mean(x * x, axis=-1, keepdims=True)
    return x * lax.rsqrt(ms + EPS) * g


def _hcols(h):
    return slice(h * HEAD_DIM, (h + 1) * HEAD_DIM)


def _slot_cols(slot, h):
    return slice(slot * GROUP_W + h * HEAD_DIM, slot * GROUP_W + (h + 1) * HEAD_DIM)


def _head_norm_gate(o, gn, gate):
    return (_rms(o, gn) * gate).astype(BF16)


def _retention_tables(lg, rows, chunk):
    row = lax.broadcasted_iota(jnp.int32, (rows, rows), 0)
    col = lax.broadcasted_iota(jnp.int32, (rows, rows), 1)
    diff = row - col
    ok = jnp.where((row ^ col) < chunk, diff, -1) >= 0
    dmask = jnp.where(ok, jnp.exp(jnp.maximum(diff, 0).astype(F32) * lg), 0.0)
    pos = (lax.broadcasted_iota(jnp.int32, (rows, HEAD_DIM), 0) & (chunk - 1)).astype(F32)
    return dmask, jnp.exp((pos + 1.0) * lg), jnp.exp((chunk - 1.0 - pos) * lg)


def _retention_head(q, k, v, gate, cos, sin, gn, dmask, qdec, kdec, state_in, chunk, nseq):
    def rope(x):
        return x * cos + pltpu.roll(x, HEAD_DIM // 2, 1) * sin

    q = rope(q)
    k = rope(k) * SB_SCALE
    vb = v.astype(BF16)
    scores = _dot_nt(q.astype(BF16), k.astype(BF16)) * dmask
    o_intra = _dot(scores.astype(BF16), vb)
    qd = (q * qdec).astype(BF16)
    kd = k * kdec
    cdec = qdec[chunk - 1:chunk, :]
    outs, states = [], []
    for b in range(nseq):
        sl = slice(b * chunk, (b + 1) * chunk)
        s_old = state_in(b)
        outs.append(o_intra[sl] + _dot(qd[sl], s_old.astype(BF16)))
        states.append(cdec * s_old + _dot(kd[sl].T.astype(BF16), vb[sl]))
    o = outs[0] if nseq == 1 else jnp.concatenate(outs, axis=0)
    return _head_norm_gate(o, gn, gate * _sigmoid(gate)), states


def _hgrn_tables(rows, chunk):
    nlev = int(math.log2(chunk))
    l = np.arange(rows)[:, None]
    m = np.arange(rows)[None, :]
    same = (l // chunk) == (m // chunk)
    x = np.bitwise_xor(l, m)
    hb = np.floor(np.log2(np.maximum(x, 1))).astype(np.int64)
    level = nlev - 1 - hb
    lv = np.where(same & (l > m), level, np.where(l == m, nlev, -1)).astype(np.float32)
    tri = (same & (l >= m)).astype(np.float32)
    return jnp.asarray(lv, BF16), jnp.asarray(np.tile(tri, (1, 3)), BF16)


def _split3(x):
    h1 = x.astype(BF16)
    r1 = x - h1.astype(F32)
    h2 = r1.astype(BF16)
    h3 = (r1 - h2.astype(F32)).astype(BF16)
    return h1, h2, h3


def _hgrn_head(gq, fl, vi, gate, lb, gn, lv, tri, gcum_ref, state_in, chunk, nseq, chained):
    rows = chunk * nseq
    nlev = int(math.log2(chunk))
    q = gq * _sigmoid(gq)
    log_lb, log1m_lb, one_m_lb = lb[0:1, :], lb[1:2, :], lb[2:3, :]
    t = jnp.exp(-jnp.abs(fl))
    r = 1.0 / (1.0 + t)
    sig_neg = jnp.where(fl >= 0.0, t * r, r)
    b = log1m_lb + (jnp.minimum(fl, 0.0) - jnp.log(1.0 + t))
    log_f = jnp.maximum(log_lb, b) + jnp.log(1.0 + jnp.exp(-jnp.abs(log_lb - b)))
    kk = one_m_lb * sig_neg
    vb = vi.astype(BF16)

    h1, h2, h3 = _split3(log_f)
    gcum = _dot(tri, jnp.concatenate([h1, h2, h3], axis=0))
    gcum_ref[...] = gcum

    ridx = lax.broadcasted_iota(jnp.int32, (rows, HEAD_DIM), 0)
    qb = q.astype(BF16)
    kb = kk.astype(BF16)
    zero_scores = jnp.zeros((rows, rows), BF16)
    scores = jnp.where(lv == nlev, _dot_nt(qb, kb).astype(BF16), zero_scores)

    def add_level(scores, level, e):
        w = jnp.exp(e).astype(BF16)
        s = _dot_nt(qb * w, kb * w)
        return jnp.where(lv == level, s.astype(BF16), scores)

    for level in range(nlev):
        blk = chunk >> level
        if blk < 16:
            break
        half = blk // 2
        pieces = [jnp.broadcast_to(gcum_ref[r0 + half - 1:r0 + half, :], (blk, HEAD_DIM))
                  for r0 in range(0, rows, blk)]
        gmid = pieces[0] if len(pieces) == 1 else jnp.concatenate(pieces, axis=0)
        scores = add_level(scores, level, -jnp.abs(gcum - gmid))

    f1 = pltpu.roll(log_f, 1, 0)
    f2 = pltpu.roll(log_f, 2, 0)
    f3 = pltpu.roll(log_f, 3, 0)
    b1 = pltpu.roll(log_f, rows - 1, 0)
    b2 = pltpu.roll(log_f, rows - 2, 0)
    b3 = pltpu.roll(log_f, rows - 3, 0)
    a1 = log_f
    a2 = a1 + f1
    a3 = a2 + f2
    a4 = a3 + f3
    c1 = b1
    c2 = c1 + b2
    c3 = c2 + b3
    zero = jnp.zeros_like(log_f)

    def pick(idx, table):
        out = table[-1]
        for j in range(len(table) - 2, -1, -1):
            out = jnp.where(idx == j, table[j], out)
        return out

    small = {8: pick(ridx & 7, [c3, c2, c1, zero, a1, a2, a3, a4]),
             4: pick(ridx & 3, [c1, zero, a1, a2]),
             2: pick(ridx & 1, [zero, a1])}
    for blk in (8, 4, 2):
        if blk <= chunk:
            scores = add_level(scores, nlev - int(math.log2(blk)), small[blk])

    o_intra = _dot(scores, vb)
    qg = (q * jnp.exp(gcum)).astype(BF16)
    outs, states = [], []
    st_chain = state_in(0) if chained else None
    for bi in range(nseq):
        sl = slice(bi * chunk, (bi + 1) * chunk)
        glast = gcum_ref[(bi + 1) * chunk - 1:(bi + 1) * chunk, :]
        st_old = st_chain if chained else state_in(bi)
        outs.append(o_intra[sl] + _dot_nt(qg[sl], st_old.astype(BF16)))
        kdec = (kk[sl] * jnp.exp(glast - gcum[sl])).astype(BF16)
        st_new = st_old * jnp.exp(glast) + _dot(vi[sl].T.astype(BF16), kdec)
        if chained:
            st_chain = st_new
        else:
            states.append(st_new)
    if chained:
        states = [st_chain]
    o = outs[0] if nseq == 1 else jnp.concatenate(outs, axis=0)
    return _head_norm_gate(o, gn, _sigmoid(gate)), states


def _pool_group(ext_ref, pw_ref, psc_ref, o_ref, pos1, seq, nseq):
    for gi, w in enumerate(POOL_WINDOWS):
        cols = _hcols(gi)
        cnt = jnp.minimum(pos1, w).astype(F32)
        ps = []
        for b in range(nseq):
            tok = ext_ref[b, POOL_HEAD:POOL_HEAD + seq, cols]
            wsum = tok
            for i in range(1, w):
                wsum = wsum + ext_ref[b, POOL_HEAD - i:POOL_HEAD - i + seq, cols]
            ps.append(wsum / cnt - tok)
        p = ps[0] if nseq == 1 else jnp.concatenate(ps, axis=0)
        y = _dot(p.astype(BF16), pw_ref[gi]) * psc_ref[:, cols]
        o_ref[:, cols] = y.astype(BF16)


_KEEP_RET, _KEEP_POOL, _KEEP_HGRN = 0, 4, 5
_N_KEEP = 9


def _in_mix_kernel(lg_ref, x_ref, g1_ref, w_ref, cos_ref, sin_ref, rgn_ref, pw_ref, psc_ref, lb_ref, hgn_ref,
                   lv_ref, tri_ref, *rest, seq, nseq, carry, pos0, n_groups):
    if carry:
        (zsb_ref, ro_ref, po_ref, ho_ref, rs_out, hs_out, pc_out,
         zr, dmask_ref, qdec_ref, kdec_ref, gcum_ref, ext_ref, rstate_ref, hstate_ref) = rest
        rs0_ref = ps0_ref = hs0_ref = None
    else:
        (rs0_ref, ps0_ref, hs0_ref, zsb_ref, ro_ref, po_ref, ho_ref, rs_out, hs_out, pc_out,
         zr, dmask_ref, qdec_ref, kdec_ref, gcum_ref, ext_ref) = rest
        rstate_ref = hstate_ref = None
    rows = seq * nseq
    ret_chunk = seq
    hg_chunk = min(HGRN_CHUNK, seq)
    g = pl.program_id(0)

    @pl.when(g == 0)
    def _():
        for h in range(N_HEADS):
            dmask_ref[h], qdec_ref[h], kdec_ref[h] = _retention_tables(lg_ref[h], rows, ret_chunk)
        if carry:
            rstate_ref[...] = jnp.zeros_like(rstate_ref)
            hstate_ref[...] = jnp.zeros_like(hstate_ref)
            ext_ref[:, 0:POOL_HEAD, :] = jnp.zeros((nseq, POOL_HEAD, GROUP_W), F32)

    hx = _rms(x_ref[...], g1_ref[...]).astype(BF16)
    slots = iter(range(N_SLOTS))

    def project(n):
        for _ in range(n):
            s = next(slots)
            zs = _dot(hx, w_ref[:, s * GROUP_W:(s + 1) * GROUP_W])
            if SB_SLOT0 <= s < SB_SLOT0 + N_SB_SLOTS:
                zsb_ref[:, (s - SB_SLOT0) * GROUP_W:(s - SB_SLOT0 + 1) * GROUP_W] = zs
            else:
                k = s if s < SB_SLOT0 else s - N_SB_SLOTS
                zr[:, k * GROUP_W:(k + 1) * GROUP_W] = zs

    project(4)
    cos = cos_ref[...]
    sin = sin_ref[...]
    for h in range(N_HEADS):
        project(1)
        state_in = (lambda b, h=h: rstate_ref[h]) if carry else (lambda b, h=h: rs0_ref[b, h])
        o, states = _retention_head(
            zr[:, _slot_cols(_KEEP_RET + 0, h)], zr[:, _slot_cols(_KEEP_RET + 1, h)],
            zr[:, _slot_cols(_KEEP_RET + 2, h)], zr[:, _slot_cols(_KEEP_RET + 3, h)],
            cos, sin, rgn_ref[:, _hcols(h)], dmask_ref[h], qdec_ref[h], kdec_ref[h], state_in, ret_chunk, nseq)
        ro_ref[:, _hcols(h)] = o
        for b, st in enumerate(states):
            if carry:
                rstate_ref[h] = st
            else:
                rs_out[b, h] = st

    project(2)
    if carry:
        pos_base = g * rows + pos0
    else:
        pos_base = pos0
    for b in range(nseq):
        if not carry:
            ext_ref[b, 0:POOL_HEAD, :] = ps0_ref[b]
        ext_ref[b, POOL_HEAD:POOL_HEAD + seq, :] = zr[b * seq:(b + 1) * seq,
                                                      _KEEP_POOL * GROUP_W:(_KEEP_POOL + 1) * GROUP_W]
    pos1 = lax.broadcasted_iota(jnp.int32, (seq, HEAD_DIM), 0) + (pos_base + 1)
    _pool_group(ext_ref, pw_ref, psc_ref, po_ref, pos1, seq, nseq)
    for b in range(nseq):
        tail = ext_ref[b, seq:seq + POOL_HEAD, :]
        if carry:
            ext_ref[b, 0:POOL_HEAD, :] = tail
            pc_out[...] = tail
        else:
            pc_out[b] = tail

    project(2)
    lv = lv_ref[...]
    tri = tri_ref[...]
    hg_nseq = rows // hg_chunk
    for h in range(N_HEADS):
        state_in = (lambda b, h=h: hstate_ref[h]) if carry else (lambda b, h=h: hs0_ref[b, h].T)
        o, states = _hgrn_head(
            zr[:, _slot_cols(_KEEP_HGRN + 0, h)], zr[:, _slot_cols(_KEEP_HGRN + 1, h)],
            zr[:, _slot_cols(_KEEP_HGRN + 2, h)], zr[:, _slot_cols(_KEEP_HGRN + 3, h)],
            lb_ref[:, _hcols(h)], hgn_ref[:, _hcols(h)], lv, tri, gcum_ref.at[h], state_in,
            hg_chunk, hg_nseq, chained=carry)
        ho_ref[:, _hcols(h)] = o
        for b, st in enumerate(states):
            if carry:
                hstate_ref[h] = st
            else:
                hs_out[b, h] = st.T

    assert next(slots, None) is None

    if carry:
        @pl.when(g == n_groups - 1)
        def _():
            rs_out[...] = rstate_ref[...]
            for h in range(N_HEADS):
                hs_out[h] = hstate_ref[h].T


def in_proj_mix(x, g1, w, cos, sin, p, lb, l, states, *, batch, seq, pos0):
    t, d = x.shape
    carry = states is None
    rows = min(ROW_GROUP, t)
    if carry:
        assert batch == 1 and t % rows == 0
        kseq, nseq = rows, 1
    else:
        assert rows % seq == 0 and batch % (rows // seq) == 0
        kseq, nseq = seq, rows // seq
    n_groups = t // rows
    hg_chunk = min(HGRN_CHUNK, kseq)
    lv, tri = _hgrn_tables(rows, hg_chunk)
    lg = jnp.asarray(np.log(1.0 - 2.0 ** (-5.0 - np.arange(N_HEADS))), F32)
    lb_tab = jnp.stack([jnp.log(lb), jnp.log1p(-lb), 1.0 - lb], axis=0)

    grp = lambda g: (g, 0)
    const = lambda g: (0, 0)
    tab_idx = grp if carry else const
    vec = pl.BlockSpec((1, GROUP_W), const)
    in_specs = [
        pl.BlockSpec(memory_space=pltpu.SMEM),
        pl.BlockSpec((rows, d), grp), pl.BlockSpec((1, d), const), pl.BlockSpec(w.shape, const),
        pl.BlockSpec((rows, HEAD_DIM), tab_idx), pl.BlockSpec((rows, HEAD_DIM), tab_idx), vec,
        pl.BlockSpec((None,) + p["pool_w"].shape[1:], lambda g: (l, 0, 0, 0)), vec,
        pl.BlockSpec((3, GROUP_W), const), vec,
        pl.BlockSpec((rows, rows), const), pl.BlockSpec((rows, 3 * rows), const),
    ]
    args = [lg, x, g1.reshape(1, d), w, cos, sin, p["ret_norm_g"][l].reshape(1, GROUP_W),
            p["pool_w"], p["pool_scale"][l].reshape(1, GROUP_W), lb_tab, p["hg_norm_g"][l].reshape(1, GROUP_W),
            lv, tri]
    mix_spec = pl.BlockSpec((rows, GROUP_W), grp)
    mix_shape = jax.ShapeDtypeStruct((t, GROUP_W), BF16)
    state_blk = (nseq, N_HEADS, HEAD_DIM, HEAD_DIM)
    if carry:
        whole = lambda shape: pl.BlockSpec(shape, lambda g: (0,) * len(shape))
        st_shape = (N_HEADS, HEAD_DIM, HEAD_DIM)
        state_specs = [whole(st_shape), whole(st_shape), whole((POOL_HEAD, GROUP_W))]
        state_shapes = [jax.ShapeDtypeStruct(st_shape, F32), jax.ShapeDtypeStruct(st_shape, F32),
                        jax.ShapeDtypeStruct((POOL_HEAD, GROUP_W), F32)]
    else:
        ret_s, pool_s, hg_s = states
        st_in = pl.BlockSpec((None,) + state_blk, lambda g: (l, g, 0, 0, 0))
        pool_in = pl.BlockSpec((None, nseq, POOL_HEAD, GROUP_W), lambda g: (l, g, 0, 0))
        in_specs += [st_in, pool_in, st_in]
        args += [ret_s, pool_s, hg_s]
        st_out = pl.BlockSpec(state_blk, lambda g: (g, 0, 0, 0))
        state_specs = [st_out, st_out, pl.BlockSpec((nseq, POOL_HEAD, GROUP_W), lambda g: (g, 0, 0))]
        state_shapes = [jax.ShapeDtypeStruct(ret_s.shape[1:], F32), jax.ShapeDtypeStruct(hg_s.shape[1:], F32),
                        jax.ShapeDtypeStruct((batch, POOL_HEAD, GROUP_W), F32)]
    scratch = [
        pltpu.VMEM((rows, _N_KEEP * GROUP_W), F32),
        pltpu.VMEM((N_HEADS, rows, rows), F32), pltpu.VMEM((N_HEADS, rows, HEAD_DIM), F32),
        pltpu.VMEM((N_HEADS, rows, HEAD_DIM), F32), pltpu.VMEM((N_HEADS, rows, HEAD_DIM), F32),
        pltpu.VMEM((nseq, POOL_HEAD + kseq, GROUP_W), F32),
    ]
    if carry:
        scratch += [pltpu.VMEM((N_HEADS, HEAD_DIM, HEAD_DIM), F32), pltpu.VMEM((N_HEADS, HEAD_DIM, HEAD_DIM), F32)]
    return pl.pallas_call(
        functools.partial(_in_mix_kernel, seq=kseq, nseq=nseq, carry=carry, pos0=pos0, n_groups=n_groups),
        grid=(n_groups,),
        in_specs=in_specs,
        out_specs=[pl.BlockSpec((rows, N_SB_SLOTS * GROUP_W), grp), mix_spec, mix_spec, mix_spec, *state_specs],
        out_shape=[jax.ShapeDtypeStruct((t, N_SB_SLOTS * GROUP_W), F32), mix_shape, mix_shape, mix_shape,
                   *state_shapes],
        scratch_shapes=scratch,
        compiler_params=_cparams("arbitrary"),
        name="in_proj_mix",
    )(*args)


def _zslot(slot):
    return lambda g: (g, slot)


def _strict_upper_ones_twice(n):
    row = lax.broadcasted_iota(jnp.int32, (2 * n, n), 0) & (n - 1)
    col = lax.broadcasted_iota(jnp.int32, (2 * n, n), 1)
    return jnp.where(row > col, 1.0, 0.0).astype(BF16)


def _sb_block(qb, kblk, vblk, carry, acc, *, diag):
    bq, bk = qb.shape[0], kblk.shape[0]
    nz = _dot_nt(qb, kblk.astype(BF16)) * (-SB_SCALE)
    lf = _log_sigmoid(nz)
    if diag:
        row = lax.broadcasted_iota(jnp.int32, (bq, bk), 0)
        col = lax.broadcasted_iota(jnp.int32, (bq, bk), 1)
        valid = col < row
        lf = jnp.where(valid, lf, 0.0)
    hi = lf.astype(BF16)
    lo = (lf - hi.astype(F32)).astype(BF16)
    later_in = _dot(jnp.concatenate([hi, lo], axis=1), _strict_upper_ones_twice(bk))
    a = jnp.exp(lf - nz + later_in + carry)
    if diag:
        a = jnp.where(valid, a, 0.0)
    acc = acc + _dot(a.astype(BF16), vblk.astype(BF16))
    carry = carry + later_in[:, 0:1] + lf[:, 0:1]
    return carry, acc


def _sb_kernel(q_ref, kd_ref, vd_ref, kp_ref, vp_ref, kfar_ref, vfar_ref, o_ref, kbuf_ref, vbuf_ref, sem_ref,
               *, bq, bk, fresh, layer, past):
    g = pl.program_id(0)
    qs = [q_ref[:, _hcols(h)].astype(BF16) for h in range(N_HEADS)]

    if fresh:
        def head_rows(ref, h):
            return ref[:, _hcols(h)]
        n_blocks = g
    else:
        def head_rows(ref, h):
            return ref[pl.ds(h, bk, stride=N_HEADS), :]
        n_blocks = past // bk

    def sweep(carries, accs, k_ref, v_ref, get, diag):
        out = [_sb_block(qs[h], get(k_ref, h), get(v_ref, h), carries[h], accs[h], diag=diag)
               for h in range(N_HEADS)]
        return tuple(c for c, _ in out), tuple(a for _, a in out)

    def alive(carries):
        m = functools.reduce(jnp.maximum, carries)
        return jnp.max(m) > SB_DEAD_LOG

    carries = tuple(jnp.zeros((bq, 1), F32) for _ in range(N_HEADS))
    accs = tuple(jnp.zeros((bq, HEAD_DIM), F32) for _ in range(N_HEADS))
    carries, accs = sweep(carries, accs, kd_ref, vd_ref, lambda ref, h: ref[:, _hcols(h)], True)

    near = sweep(carries, accs, kp_ref, vp_ref, head_rows, False)
    if fresh:
        has_past = g >= 1
        carries, accs = jax.tree.map(lambda new, old: jnp.where(has_past, new, old), near, (carries, accs))
    else:
        carries, accs = near

    def cond(st):
        c, carries, _ = st
        return jnp.logical_and(c < n_blocks, alive(carries))

    def body(st):
        c, carries, accs = st
        if fresh:
            start = pl.multiple_of((g - 1 - c) * bk, bk)
            k_src = kfar_ref.at[pl.ds(start, bk), pl.ds(1 * GROUP_W, GROUP_W)]
            v_src = vfar_ref.at[pl.ds(start, bk), pl.ds(2 * GROUP_W, GROUP_W)]
        else:
            start = pl.multiple_of((past - (c + 1) * bk) * N_HEADS, bk * N_HEADS)
            k_src = kfar_ref.at[layer, g, pl.ds(start, bk * N_HEADS), :]
            v_src = vfar_ref.at[layer, g, pl.ds(start, bk * N_HEADS), :]
        k_copy = pltpu.make_async_copy(k_src, kbuf_ref, sem_ref.at[0])
        v_copy = pltpu.make_async_copy(v_src, vbuf_ref, sem_ref.at[1])
        k_copy.start()
        v_copy.start()
        k_copy.wait()
        v_copy.wait()
        carries, accs = sweep(carries, accs, kbuf_ref, vbuf_ref, head_rows, False)
        return c + 1, carries, accs

    _, _, accs = lax.while_loop(cond, body, (jnp.int32(1), carries, accs))
    for h in range(N_HEADS):
        o_ref[:, _hcols(h)] = accs[h].astype(BF16)


def stick_breaking(zsb, k_past, v_past, layer, *, batch, seq):
    t = zsb.shape[0]
    fresh = k_past is None
    if fresh:
        assert batch == 1
        bq = bk = min(ROW_GROUP, t)
        past = None
        prev_idx = lambda slot: (lambda g: (jnp.maximum(g - 1, 0), slot))
        past_specs = [pl.BlockSpec((bk, GROUP_W), prev_idx(1)), pl.BlockSpec((bk, GROUP_W), prev_idx(2))]
        k_far, v_far = zsb, zsb
        buf_shape = (bk, GROUP_W)
    else:
        bq = seq
        past = k_past.shape[2] // N_HEADS
        bk = min(ROW_GROUP, past)
        assert past % bk == 0
        last = past // bk - 1
        tail = pl.BlockSpec((None, None, bk * N_HEADS, HEAD_DIM), lambda g: (layer, g, last, 0))
        past_specs = [tail, tail]
        k_far, v_far = k_past, v_past
        buf_shape = (bk * N_HEADS, HEAD_DIM)
    blk = (bq, GROUP_W)
    any_spec = pl.BlockSpec(memory_space=pl.ANY)
    return pl.pallas_call(
        functools.partial(_sb_kernel, bq=bq, bk=bk, fresh=fresh, layer=layer, past=past),
        grid=(t // bq,),
        in_specs=[pl.BlockSpec(blk, _zslot(0)), pl.BlockSpec(blk, _zslot(1)), pl.BlockSpec(blk, _zslot(2)),
                  *past_specs, any_spec, any_spec],
        out_specs=pl.BlockSpec(blk, lambda g: (g, 0)),
        out_shape=jax.ShapeDtypeStruct((t, GROUP_W), BF16),
        scratch_shapes=[pltpu.VMEM(buf_shape, F32), pltpu.VMEM(buf_shape, F32), pltpu.SemaphoreType.DMA((2,))],
        compiler_params=_cparams("arbitrary"),
        name="stick_breaking",
    )(zsb, zsb, zsb, k_far, v_far, k_far, v_far)


def _emit_kv_kernel(*refs, depth, rows):
    k_refs, v_refs = refs[:depth], refs[depth:2 * depth]
    ko_ref, vo_ref = refs[2 * depth:]
    for l in range(depth):
        for h in range(N_HEADS):
            ko_ref[l, pl.ds(h, rows, stride=N_HEADS), :] = k_refs[l][:, _hcols(h)]
            vo_ref[l, pl.ds(h, rows, stride=N_HEADS), :] = v_refs[l][:, _hcols(h)]


def emit_kv(zsbs, *, rows=ROW_GROUP):
    depth = len(zsbs)
    t = zsbs[0].shape[0]
    rows = min(rows, t)
    out_shape = jax.ShapeDtypeStruct((depth, t * N_HEADS, HEAD_DIM), F32)
    out_spec = pl.BlockSpec((depth, rows * N_HEADS, HEAD_DIM), lambda g: (0, g, 0))
    blk = (rows, GROUP_W)
    return pl.pallas_call(
        functools.partial(_emit_kv_kernel, depth=depth, rows=rows),
        grid=(t // rows,),
        in_specs=[pl.BlockSpec(blk, _zslot(1))] * depth + [pl.BlockSpec(blk, _zslot(2))] * depth,
        out_specs=[out_spec, out_spec],
        out_shape=[out_shape, out_shape],
        compiler_params=_cparams("parallel"),
        name="emit_kv",
    )(*zsbs, *zsbs)


def _out_proj_kernel(x_ref, m0_ref, m1_ref, m2_ref, m3_ref, w_ref, o_ref):
    acc = x_ref[...]
    for gi, m_ref in enumerate((m0_ref, m1_ref, m2_ref, m3_ref)):
        acc = acc + _dot(m_ref[...], w_ref[gi * GROUP_W:(gi + 1) * GROUP_W, :])
    o_ref[...] = acc


def out_proj(x, mixes, w, layer, *, tm=512):
    t, d = x.shape
    tm = min(tm, t)
    mix_spec = pl.BlockSpec((tm, GROUP_W), lambda i: (i, 0))
    return pl.pallas_call(
        _out_proj_kernel,
        grid=(t // tm,),
        in_specs=[pl.BlockSpec((tm, d), lambda i: (i, 0)), mix_spec, mix_spec, mix_spec, mix_spec,
                  pl.BlockSpec((None,) + w.shape[1:], lambda i: (layer, 0, 0), pipeline_mode=pl.Buffered(1))],
        out_specs=pl.BlockSpec((tm, d), lambda i: (i, 0)),
        out_shape=jax.ShapeDtypeStruct((t, d), F32),
        compiler_params=_cparams("parallel"),
        name="out_proj",
    )(x, *mixes, w)


def _mlp_kernel(x_ref, g_ref, wu_ref, wd_ref, *rest, final, n_cast):
    rest = list(rest)
    fg_ref = rest.pop(0) if final else None
    cast_in = [rest.pop(0) for _ in range(n_cast)]
    o_ref = rest.pop(0)
    cast_out = [rest.pop(0) for _ in range(n_cast)]
    (h_ref,) = rest
    f = pl.program_id(1)

    @pl.when(f == 0)
    def _():
        x = x_ref[...]
        h_ref[...] = _rms(x, g_ref[...]).astype(BF16)
        o_ref[...] = x

    u = _dot(h_ref[...], wu_ref[...])
    a = jnp.square(jnp.maximum(u, 0.0)).astype(BF16)
    o_ref[...] += _dot(a, wd_ref[...])

    for src_ref, dst_ref in zip(cast_in, cast_out):
        dst_ref[...] = src_ref[...].astype(BF16)

    if final:
        @pl.when(f == pl.num_programs(1) - 1)
        def _():
            o_ref[...] = _rms(o_ref[...], fg_ref[...])


def mlp(x, g, w_up, w_down, final_g=None, cast_next=None, *, tm=1024, tf=512):
    t, d = x.shape
    dff = w_up.shape[1]
    tm = min(tm, t)
    nf = dff // tf
    steps = (t // tm) * nf
    final = final_g is not None
    vec_spec = pl.BlockSpec((1, d), lambda i, f: (0, 0))
    in_specs = [pl.BlockSpec((tm, d), lambda i, f: (i, 0)), vec_spec,
                pl.BlockSpec((d, tf), lambda i, f: (0, f)), pl.BlockSpec((tf, d), lambda i, f: (f, 0))]
    args = [x, g.reshape(1, d), w_up, w_down]
    if final:
        in_specs.append(vec_spec)
        args.append(final_g.reshape(1, d))
    out_specs = [pl.BlockSpec((tm, d), lambda i, f: (i, 0))]
    out_shape = [jax.ShapeDtypeStruct((t, d), F32)]
    n_cast = 0
    if cast_next is not None:
        layer, stacked = cast_next
        n_cast = len(stacked)
        for w in stacked:
            _, r, c = w.shape
            slab = r // steps
            assert slab * steps == r and slab % 16 == 0, (w.shape, steps)
            in_specs.append(pl.BlockSpec((None, slab, c), lambda i, f: (layer, i * nf + f, 0)))
            args.append(w)
            out_specs.append(pl.BlockSpec((slab, c), lambda i, f: (i * nf + f, 0)))
            out_shape.append(jax.ShapeDtypeStruct((r, c), BF16))
    out = pl.pallas_call(
        functools.partial(_mlp_kernel, final=final, n_cast=n_cast),
        grid=(t // tm, nf),
        in_specs=in_specs,
        out_specs=out_specs,
        out_shape=out_shape,
        scratch_shapes=[pltpu.VMEM((tm, d), BF16)],
        compiler_params=_cparams("parallel", "arbitrary"),
        name="mlp",
    )(*args)
    return (out[0], out[1:]) if n_cast else out[0]


def _rope_tables(pos):
    half = HEAD_DIM // 2
    inv = ROPE_BASE ** (-jnp.arange(half, dtype=F32) / half)
    ang = pos.astype(F32)[:, None] * inv[None, :]
    cos = jnp.cos(ang)
    sin = jnp.sin(ang)
    return jnp.concatenate([cos, cos], axis=-1), jnp.concatenate([-sin, sin], axis=-1)


class _Group:
    def __init__(self, x, pos0, states, depth):
        self.batch, self.seq, self.d = x.shape
        self.pos0 = pos0
        t = self.batch * self.seq
        self.x2 = x.reshape(t, self.d)
        rows = min(ROW_GROUP, t)
        self.cos, self.sin = _rope_tables(pos0 + jnp.arange(self.seq, dtype=jnp.int32))
        if states is None:
            self.mix_states = self.k_past = self.v_past = None
        else:
            reps = rows // self.seq
            self.cos, self.sin = jnp.tile(self.cos, (reps, 1)), jnp.tile(self.sin, (reps, 1))
            past = states[1].shape[2]
            self.k_past = states[1].reshape(depth, self.batch, past * N_HEADS, HEAD_DIM)
            self.v_past = states[2].reshape(depth, self.batch, past * N_HEADS, HEAD_DIM)
            pool_s = jnp.pad(states[3], ((0, 0), (0, 0), (POOL_HEAD - POOL_PAD, 0), (0, 0)))
            self.mix_states = (states[0], pool_s, states[4])
        self.zsbs, self.rets, self.pools, self.hgs = [], [], [], []

    def layer(self, l, p, lb, wb, final_g, cast_next):
        bs = dict(batch=self.batch, seq=self.seq)
        zsb, ro, po, ho, ret_new, hg_new, pool_rows = in_proj_mix(
            self.x2, p["norm1_g"][l], wb[0], self.cos, self.sin, p, lb, l, self.mix_states,
            pos0=self.pos0, **bs)
        so = stick_breaking(zsb, self.k_past, self.v_past, l, **bs)
        x1 = out_proj(self.x2, (ro, so, po, ho), p["w_out"], l)
        out = mlp(x1, p["norm2_g"][l], wb[1], wb[2], final_g, cast_next)
        self.x2, casts = out if cast_next is not None else (out, None)
        state_shape = (self.batch, N_HEADS, HEAD_DIM, HEAD_DIM)
        self.zsbs.append(zsb)
        self.rets.append(ret_new.reshape(state_shape))
        self.hgs.append(hg_new.reshape(state_shape))
        self.pools.append(pool_rows.reshape(self.batch, POOL_HEAD, GROUP_W)[:, POOL_HEAD - POOL_PAD:, :])
        return casts

    def outputs(self):
        ks, vs = emit_kv(self.zsbs)
        kv_shape = (len(self.zsbs), self.batch, self.seq, N_HEADS, HEAD_DIM)
        y = self.x2.reshape(self.batch, self.seq, self.d)
        return (y, jnp.stack(self.rets), ks.reshape(kv_shape), vs.reshape(kv_shape),
                jnp.stack(self.pools), jnp.stack(self.hgs))


def kernel(x_prompt, x_sample, state_ret, cache_sb_k, cache_sb_v, state_pool, state_hgrn, norm1_g, w_in, ret_norm_g, pool_w, pool_scale, hg_lower_bounds, hg_norm_g, w_out, norm2_g, w_up, w_down, final_norm_g):
    lb_all = jnp.cumsum(jax.nn.softmax(hg_lower_bounds.astype(F32), axis=0), axis=0)
    lb_all = lb_all - lb_all[0:1]
    depth = w_in.shape[0]
    params = dict(norm1_g=norm1_g, ret_norm_g=ret_norm_g, pool_w=pool_w.astype(BF16), pool_scale=pool_scale,
                  hg_norm_g=hg_norm_g, w_out=w_out.astype(BF16), norm2_g=norm2_g)
    big = [w_in, w_up, w_down]
    prompt = _Group(x_prompt, 0, None, depth)
    streams = _Group(x_sample, cache_sb_k.shape[2],
                     (state_ret, cache_sb_k, cache_sb_v, state_pool, state_hgrn), depth)
    wb = tuple(w[0].astype(BF16) for w in big)
    for l in range(depth):
        final_g = final_norm_g if l == depth - 1 else None
        cast_next = (l + 1, big) if l + 1 < depth else None
        wb_next = prompt.layer(l, params, lb_all[l], wb, final_g, cast_next)
        streams.layer(l, params, lb_all[l], wb, final_g, None)
        wb = wb_next
    y_p, ret_p, sbk_p, sbv_p, pool_p, hg_p = prompt.outputs()
    y_s, ret_s, sbk_s, sbv_s, pool_s, hg_s = streams.outputs()
    return (y_p, y_s, ret_p, ret_s, sbk_p, sbv_p, sbk_s, sbv_s, pool_p, pool_s, hg_p, hg_s)
```

```python
import functools
import math

import numpy as np
import jax
import jax.numpy as jnp
from jax import lax
from jax.experimental import pallas as pl
from jax.experimental.pallas import tpu as pltpu

F32 = jnp.float32
BF16 = jnp.bfloat16

HEAD_DIM = 128
N_HEADS = 4
GROUP_W = N_HEADS * HEAD_DIM
N_SLOTS = 12
SB_SLOT0, N_SB_SLOTS = 4, 3
POOL_WINDOWS = (2, 4, 8, 16)
POOL_PAD = max(POOL_WINDOWS) - 1
POOL_HEAD = 16
ROPE_BASE = 10000.0
EPS = 1e-6
SB_SCALE = HEAD_DIM ** -0.5
SB_DEAD_LOG = -104.0

VMEM_LIMIT = 60 * 1024 * 1024
ROW_GROUP = 256
HGRN_CHUNK = 64


def _cparams(*sem):
    return pltpu.CompilerParams(dimension_semantics=sem, vmem_limit_bytes=VMEM_LIMIT)


def _dot(a, b):
    return jnp.dot(a, b, preferred_element_type=F32)


def _dot_nt(a, b):
    return lax.dot_general(a, b, (((1,), (1,)), ((), ())), preferred_element_type=F32)


def _sigmoid(x):
    return 1.0 / (1.0 + jnp.exp(-x))


def _log_sigmoid(x):
    return jnp.minimum(x, 0.0) - jnp.log(1.0 + jnp.exp(-jnp.abs(x)))


def _rms(x, g):
    ms = jnp.mean(x * x, axis=-1, keepdims=True)
    return x * lax.rsqrt(ms + EPS) * g


def _hcols(h):
    return slice(h * HEAD_DIM, (h + 1) * HEAD_DIM)


def _slot_cols(slot, h):
    return slice(slot * GROUP_W + h * HEAD_DIM, slot * GROUP_W + (h + 1) * HEAD_DIM)


def _head_norm_gate(o, gn, gate):
    return (_rms(o, gn) * gate).astype(BF16)


def _retention_tables(lg, rows, chunk):
    row = lax.broadcasted_iota(jnp.int32, (rows, rows), 0)
    col = lax.broadcasted_iota(jnp.int32, (rows, rows), 1)
    diff = row - col
    ok = jnp.where((row ^ col) < chunk, diff, -1) >= 0
    dmask = jnp.where(ok, jnp.exp(jnp.maximum(diff, 0).astype(F32) * lg), 0.0)
    pos = (lax.broadcasted_iota(jnp.int32, (rows, HEAD_DIM), 0) & (chunk - 1)).astype(F32)
    return dmask, jnp.exp((pos + 1.0) * lg), jnp.exp((chunk - 1.0 - pos) * lg)


def _retention_head(q, k, v, gate, cos, sin, gn, dmask, qdec, kdec, state_in, chunk, nseq):
    def rope(x):
        return x * cos + pltpu.roll(x, HEAD_DIM // 2, 1) * sin

    q = rope(q)
    k = rope(k) * SB_SCALE
    vb = v.astype(BF16)
    scores = _dot_nt(q.astype(BF16), k.astype(BF16)) * dmask
    o_intra = _dot(scores.astype(BF16), vb)
    qd = (q * qdec).astype(BF16)
    kd = k * kdec
    cdec = qdec[chunk - 1:chunk, :]
    outs, states = [], []
    for b in range(nseq):
        sl = slice(b * chunk, (b + 1) * chunk)
        s_old = state_in(b)
        outs.append(o_intra[sl] + _dot(qd[sl], s_old.astype(BF16)))
        states.append(cdec * s_old + _dot(kd[sl].T.astype(BF16), vb[sl]))
    o = outs[0] if nseq == 1 else jnp.concatenate(outs, axis=0)
    return _head_norm_gate(o, gn, gate * _sigmoid(gate)), states


def _hgrn_tables(rows, chunk):
    nlev = int(math.log2(chunk))
    l = np.arange(rows)[:, None]
    m = np.arange(rows)[None, :]
    same = (l // chunk) == (m // chunk)
    x = np.bitwise_xor(l, m)
    hb = np.floor(np.log2(np.maximum(x, 1))).astype(np.int64)
    level = nlev - 1 - hb
    lv = np.where(same & (l > m), level, np.where(l == m, nlev, -1)).astype(np.float32)
    tri = (same & (l >= m)).astype(np.float32)
    return jnp.asarray(lv, BF16), jnp.asarray(np.tile(tri, (1, 3)), BF16)


def _split3(x):
    h1 = x.astype(BF16)
    r1 = x - h1.astype(F32)
    h2 = r1.astype(BF16)
    h3 = (r1 - h2.astype(F32)).astype(BF16)
    return h1, h2, h3


def _hgrn_head(gq, fl, vi, gate, lb, gn, lv, tri, gcum_ref, state_in, chunk, nseq, chained):
    rows = chunk * nseq
    nlev = int(math.log2(chunk))
    q = gq * _sigmoid(gq)
    log_lb, log1m_lb, one_m_lb = lb[0:1, :], lb[1:2, :], lb[2:3, :]
    t = jnp.exp(-jnp.abs(fl))
    r = 1.0 / (1.0 + t)
    sig_neg = jnp.where(fl >= 0.0, t * r, r)
    b = log1m_lb + (jnp.minimum(fl, 0.0) - jnp.log(1.0 + t))
    log_f = jnp.maximum(log_lb, b) + jnp.log(1.0 + jnp.exp(-jnp.abs(log_lb - b)))
    kk = one_m_lb * sig_neg
    vb = vi.astype(BF16)

    h1, h2, h3 = _split3(log_f)
    gcum = _dot(tri, jnp.concatenate([h1, h2, h3], axis=0))
    gcum_ref[...] = gcum

    ridx = lax.broadcasted_iota(jnp.int32, (rows, HEAD_DIM), 0)
    qb = q.astype(BF16)
    kb = kk.astype(BF16)
    zero_scores = jnp.zeros((rows, rows), BF16)
    scores = jnp.where(lv == nlev, _dot_nt(qb, kb).astype(BF16), zero_scores)

    def add_level(scores, level, e):
        w = jnp.exp(e).astype(BF16)
        s = _dot_nt(qb * w, kb * w)
        return jnp.where(lv == level, s.astype(BF16), scores)

    for level in range(nlev):
        blk = chunk >> level
        if blk < 16:
            break
        half = blk // 2
        pieces = [jnp.broadcast_to(gcum_ref[r0 + half - 1:r0 + half, :], (blk, HEAD_DIM))
                  for r0 in range(0, rows, blk)]
        gmid = pieces[0] if len(pieces) == 1 else jnp.concatenate(pieces, axis=0)
        scores = add_level(scores, level, -jnp.abs(gcum - gmid))

    f1 = pltpu.roll(log_f, 1, 0)
    f2 = pltpu.roll(log_f, 2, 0)
    f3 = pltpu.roll(log_f, 3, 0)
    b1 = pltpu.roll(log_f, rows - 1, 0)
    b2 = pltpu.roll(log_f, rows - 2, 0)
    b3 = pltpu.roll(log_f, rows - 3, 0)
    a1 = log_f
    a2 = a1 + f1
    a3 = a2 + f2
    a4 = a3 + f3
    c1 = b1
    c2 = c1 + b2
    c3 = c2 + b3
    zero = jnp.zeros_like(log_f)

    def pick(idx, table):
        out = table[-1]
        for j in range(len(table) - 2, -1, -1):
            out = jnp.where(idx == j, table[j], out)
        return out

    small = {8: pick(ridx & 7, [c3, c2, c1, zero, a1, a2, a3, a4]),
             4: pick(ridx & 3, [c1, zero, a1, a2]),
             2: pick(ridx & 1, [zero, a1])}
    for blk in (8, 4, 2):
        if blk <= chunk:
            scores = add_level(scores, nlev - int(math.log2(blk)), small[blk])

    o_intra = _dot(scores, vb)
    qg = (q * jnp.exp(gcum)).astype(BF16)
    outs, states = [], []
    st_chain = state_in(0) if chained else None
    for bi in range(nseq):
        sl = slice(bi * chunk, (bi + 1) * chunk)
        glast = gcum_ref[(bi + 1) * chunk - 1:(bi + 1) * chunk, :]
        st_old = st_chain if chained else state_in(bi)
        outs.append(o_intra[sl] + _dot_nt(qg[sl], st_old.astype(BF16)))
        kdec = (kk[sl] * jnp.exp(glast - gcum[sl])).astype(BF16)
        st_new = st_old * jnp.exp(glast) + _dot(vi[sl].T.astype(BF16), kdec)
        if chained:
            st_chain = st_new
        else:
            states.append(st_new)
    if chained:
        states = [st_chain]
    o = outs[0] if nseq == 1 else jnp.concatenate(outs, axis=0)
    return _head_norm_gate(o, gn, _sigmoid(gate)), states


def _pool_group(ext_ref, pw_ref, psc_ref, o_ref, pos1, seq, nseq):
    for gi, w in enumerate(POOL_WINDOWS):
        cols = _hcols(gi)
        cnt = jnp.minimum(pos1, w).astype(F32)
        ps = []
        for b in range(nseq):
            tok = ext_ref[b, POOL_HEAD:POOL_HEAD + seq, cols]
            wsum = tok
            for i in range(1, w):
                wsum = wsum + ext_ref[b, POOL_HEAD - i:POOL_HEAD - i + seq, cols]
            ps.append(wsum / cnt - tok)
        p = ps[0] if nseq == 1 else jnp.concatenate(ps, axis=0)
        y = _dot(p.astype(BF16), pw_ref[gi]) * psc_ref[:, cols]
        o_ref[:, cols] = y.astype(BF16)


_KEEP_RET, _KEEP_POOL, _KEEP_HGRN = 0, 4, 5
_N_KEEP = 9


def _in_mix_kernel(lg_ref, x_ref, g1_ref, w_ref, cos_ref, sin_ref, rgn_ref, pw_ref, psc_ref, lb_ref, hgn_ref,
                   lv_ref, tri_ref, *rest, seq, nseq, carry, pos0, n_groups, kv_alias, layer):
    rest = list(rest)
    if carry:
        rs0_ref = ps0_ref = hs0_ref = None
    else:
        rs0_ref, ps0_ref, hs0_ref = rest[:3]
        del rest[:3]
    if kv_alias:
        del rest[:2]
    zq_ref, ko_ref, vo_ref, ro_ref, po_ref, ho_ref, rs_out, hs_out, pc_out = rest[:9]
    zr, dmask_ref, qdec_ref, kdec_ref, gcum_ref, ext_ref = rest[9:15]
    rstate_ref, hstate_ref = rest[15:] if carry else (None, None)
    rows = seq * nseq
    ret_chunk = seq
    hg_chunk = min(HGRN_CHUNK, seq)
    g = pl.program_id(0)

    @pl.when(g == 0)
    def _():
        for h in range(N_HEADS):
            dmask_ref[h], qdec_ref[h], kdec_ref[h] = _retention_tables(lg_ref[h], rows, ret_chunk)
        if carry:
            rstate_ref[...] = jnp.zeros_like(rstate_ref)
            hstate_ref[...] = jnp.zeros_like(hstate_ref)
            ext_ref[:, 0:POOL_HEAD, :] = jnp.zeros((nseq, POOL_HEAD, GROUP_W), F32)

    hx = _rms(x_ref[...], g1_ref[...]).astype(BF16)
    slots = iter(range(N_SLOTS))

    def project(n):
        for _ in range(n):
            s = next(slots)
            zs = _dot(hx, w_ref[:, s * GROUP_W:(s + 1) * GROUP_W])
            if s == SB_SLOT0:
                zq_ref[...] = zs
            elif SB_SLOT0 < s < SB_SLOT0 + N_SB_SLOTS:
                kv_ref = ko_ref if s == SB_SLOT0 + 1 else vo_ref
                if not kv_alias:
                    for other in range(kv_ref.shape[0]):
                        if other != layer:
                            kv_ref[other] = jnp.zeros(kv_ref.shape[1:], F32)
                    kv_ref = kv_ref.at[layer]
                for h in range(N_HEADS):
                    kv_ref[pl.ds(h, rows, stride=N_HEADS), :] = zs[:, _hcols(h)]
            else:
                k = s if s < SB_SLOT0 else s - N_SB_SLOTS
                zr[:, k * GROUP_W:(k + 1) * GROUP_W] = zs

    project(4)
    cos = cos_ref[...]
    sin = sin_ref[...]
    for h in range(N_HEADS):
        project(1)
        state_in = (lambda b, h=h: rstate_ref[h]) if carry else (lambda b, h=h: rs0_ref[b, h])
        o, states = _retention_head(
            zr[:, _slot_cols(_KEEP_RET + 0, h)], zr[:, _slot_cols(_KEEP_RET + 1, h)],
            zr[:, _slot_cols(_KEEP_RET + 2, h)], zr[:, _slot_cols(_KEEP_RET + 3, h)],
            cos, sin, rgn_ref[:, _hcols(h)], dmask_ref[h], qdec_ref[h], kdec_ref[h], state_in, ret_chunk, nseq)
        ro_ref[:, _hcols(h)] = o
        for b, st in enumerate(states):
            if carry:
                rstate_ref[h] = st
            else:
                rs_out[b, h] = st

    project(2)
    if carry:
        pos_base = g * rows + pos0
    else:
        pos_base = pos0
    for b in range(nseq):
        if not carry:
            ext_ref[b, 0:POOL_HEAD, :] = ps0_ref[b]
        ext_ref[b, POOL_HEAD:POOL_HEAD + seq, :] = zr[b * seq:(b + 1) * seq,
                                                      _KEEP_POOL * GROUP_W:(_KEEP_POOL + 1) * GROUP_W]
    pos1 = lax.broadcasted_iota(jnp.int32, (seq, HEAD_DIM), 0) + (pos_base + 1)
    _pool_group(ext_ref, pw_ref, psc_ref, po_ref, pos1, seq, nseq)
    for b in range(nseq):
        tail = ext_ref[b, seq:seq + POOL_HEAD, :]
        if carry:
            ext_ref[b, 0:POOL_HEAD, :] = tail
            pc_out[...] = tail
        else:
            pc_out[b] = tail

    project(2)
    lv = lv_ref[...]
    tri = tri_ref[...]
    hg_nseq = rows // hg_chunk
    for h in range(N_HEADS):
        state_in = (lambda b, h=h: hstate_ref[h]) if carry else (lambda b, h=h: hs0_ref[b, h].T)
        o, states = _hgrn_head(
            zr[:, _slot_cols(_KEEP_HGRN + 0, h)], zr[:, _slot_cols(_KEEP_HGRN + 1, h)],
            zr[:, _slot_cols(_KEEP_HGRN + 2, h)], zr[:, _slot_cols(_KEEP_HGRN + 3, h)],
            lb_ref[:, _hcols(h)], hgn_ref[:, _hcols(h)], lv, tri, gcum_ref.at[h], state_in,
            hg_chunk, hg_nseq, chained=carry)
        ho_ref[:, _hcols(h)] = o
        for b, st in enumerate(states):
            if carry:
                hstate_ref[h] = st
            else:
                hs_out[b, h] = st.T

    assert next(slots, None) is None

    if carry:
        @pl.when(g == n_groups - 1)
        def _():
            rs_out[...] = rstate_ref[...]
            for h in range(N_HEADS):
                hs_out[h] = hstate_ref[h].T


def in_proj_mix(x, g1, w, cos, sin, p, lb, l, states, kv_stack, *, depth, batch, seq, pos0):
    t, d = x.shape
    carry = states is None
    rows = min(ROW_GROUP, t)
    if carry:
        assert batch == 1 and t % rows == 0
        kseq, nseq = rows, 1
    else:
        assert rows % seq == 0 and batch % (rows // seq) == 0
        kseq, nseq = seq, rows // seq
    n_groups = t // rows
    hg_chunk = min(HGRN_CHUNK, kseq)
    lv, tri = _hgrn_tables(rows, hg_chunk)
    lg = jnp.asarray(np.log(1.0 - 2.0 ** (-5.0 - np.arange(N_HEADS))), F32)
    lb_tab = jnp.stack([jnp.log(lb), jnp.log1p(-lb), 1.0 - lb], axis=0)

    grp = lambda g: (g, 0)
    const = lambda g: (0, 0)
    tab_idx = grp if carry else const
    vec = pl.BlockSpec((1, GROUP_W), const)
    in_specs = [
        pl.BlockSpec(memory_space=pltpu.SMEM),
        pl.BlockSpec((rows, d), grp), pl.BlockSpec((1, d), const), pl.BlockSpec(w.shape, const),
        pl.BlockSpec((rows, HEAD_DIM), tab_idx), pl.BlockSpec((rows, HEAD_DIM), tab_idx), vec,
        pl.BlockSpec((None,) + p["pool_w"].shape[1:], lambda g: (l, 0, 0, 0)), vec,
        pl.BlockSpec((3, GROUP_W), const), vec,
        pl.BlockSpec((rows, rows), const), pl.BlockSpec((rows, 3 * rows), const),
    ]
    args = [lg, x, g1.reshape(1, d), w, cos, sin, p["ret_norm_g"][l].reshape(1, GROUP_W),
            p["pool_w"], p["pool_scale"][l].reshape(1, GROUP_W), lb_tab, p["hg_norm_g"][l].reshape(1, GROUP_W),
            lv, tri]
    mix_spec = pl.BlockSpec((rows, GROUP_W), grp)
    mix_shape = jax.ShapeDtypeStruct((t, GROUP_W), BF16)
    state_blk = (nseq, N_HEADS, HEAD_DIM, HEAD_DIM)
    if carry:
        whole = lambda shape: pl.BlockSpec(shape, lambda g: (0,) * len(shape))
        st_shape = (N_HEADS, HEAD_DIM, HEAD_DIM)
        state_specs = [whole(st_shape), whole(st_shape), whole((POOL_HEAD, GROUP_W))]
        state_shapes = [jax.ShapeDtypeStruct(st_shape, F32), jax.ShapeDtypeStruct(st_shape, F32),
                        jax.ShapeDtypeStruct((POOL_HEAD, GROUP_W), F32)]
    else:
        ret_s, pool_s, hg_s = states
        st_in = pl.BlockSpec((None,) + state_blk, lambda g: (l, g, 0, 0, 0))
        pool_in = pl.BlockSpec((None, nseq, POOL_HEAD, GROUP_W), lambda g: (l, g, 0, 0))
        in_specs += [st_in, pool_in, st_in]
        args += [ret_s, pool_s, hg_s]
        st_out = pl.BlockSpec(state_blk, lambda g: (g, 0, 0, 0))
        state_specs = [st_out, st_out, pl.BlockSpec((nseq, POOL_HEAD, GROUP_W), lambda g: (g, 0, 0))]
        state_shapes = [jax.ShapeDtypeStruct(ret_s.shape[1:], F32), jax.ShapeDtypeStruct(hg_s.shape[1:], F32),
                        jax.ShapeDtypeStruct((batch, POOL_HEAD, GROUP_W), F32)]
    scratch = [
        pltpu.VMEM((rows, _N_KEEP * GROUP_W), F32),
        pltpu.VMEM((N_HEADS, rows, rows), F32), pltpu.VMEM((N_HEADS, rows, HEAD_DIM), F32),
        pltpu.VMEM((N_HEADS, rows, HEAD_DIM), F32), pltpu.VMEM((N_HEADS, rows, HEAD_DIM), F32),
        pltpu.VMEM((nseq, POOL_HEAD + kseq, GROUP_W), F32),
    ]
    if carry:
        scratch += [pltpu.VMEM((N_HEADS, HEAD_DIM, HEAD_DIM), F32), pltpu.VMEM((N_HEADS, HEAD_DIM, HEAD_DIM), F32)]
    kv_alias = kv_stack is not None
    aliases = {}
    if kv_alias:
        aliases = {len(args): 1, len(args) + 1: 2}
        in_specs += [pl.BlockSpec(memory_space=pl.ANY)] * 2
        args += list(kv_stack)
    kv_shape = jax.ShapeDtypeStruct((depth, t * N_HEADS, HEAD_DIM), F32)
    if kv_alias:
        kv_spec = pl.BlockSpec((None, rows * N_HEADS, HEAD_DIM), lambda g: (l, g, 0))
    else:
        kv_spec = pl.BlockSpec((depth, rows * N_HEADS, HEAD_DIM), lambda g: (0, g, 0))
    return pl.pallas_call(
        functools.partial(_in_mix_kernel, seq=kseq, nseq=nseq, carry=carry, pos0=pos0, n_groups=n_groups,
                          kv_alias=kv_alias, layer=l),
        grid=(n_groups,),
        in_specs=in_specs,
        out_specs=[pl.BlockSpec((rows, GROUP_W), grp), kv_spec, kv_spec, mix_spec, mix_spec, mix_spec,
                   *state_specs],
        out_shape=[jax.ShapeDtypeStruct((t, GROUP_W), F32), kv_shape, kv_shape, mix_shape, mix_shape, mix_shape,
                   *state_shapes],
        scratch_shapes=scratch,
        input_output_aliases=aliases,
        compiler_params=_cparams("arbitrary"),
        name="in_proj_mix",
    )(*args)


def _strict_upper_ones_twice(n):
    row = lax.broadcasted_iota(jnp.int32, (2 * n, n), 0) & (n - 1)
    col = lax.broadcasted_iota(jnp.int32, (2 * n, n), 1)
    return jnp.where(row > col, 1.0, 0.0).astype(BF16)


def _sb_block(qb, kblk, vblk, carry, acc, *, diag):
    bq, bk = qb.shape[0], kblk.shape[0]
    nz = _dot_nt(qb, kblk.astype(BF16)) * (-SB_SCALE)
    lf = _log_sigmoid(nz)
    if diag:
        row = lax.broadcasted_iota(jnp.int32, (bq, bk), 0)
        col = lax.broadcasted_iota(jnp.int32, (bq, bk), 1)
        valid = col < row
        lf = jnp.where(valid, lf, 0.0)
    hi = lf.astype(BF16)
    lo = (lf - hi.astype(F32)).astype(BF16)
    later_in = _dot(jnp.concatenate([hi, lo], axis=1), _strict_upper_ones_twice(bk))
    a = jnp.exp(lf - nz + later_in + carry)
    if diag:
        a = jnp.where(valid, a, 0.0)
    acc = acc + _dot(a.astype(BF16), vblk.astype(BF16))
    carry = carry + later_in[:, 0:1] + lf[:, 0:1]
    return carry, acc


def _sb_kernel(q_ref, kd_ref, vd_ref, kp_ref, vp_ref, kfar_ref, vfar_ref, o_ref, kbuf_ref, vbuf_ref, sem_ref,
               *, bq, bk, fresh, layer, past):
    g = pl.program_id(0)
    qs = [q_ref[:, _hcols(h)].astype(BF16) for h in range(N_HEADS)]

    def head_rows(n):
        return lambda ref, h: ref[pl.ds(h, n, stride=N_HEADS), :]

    n_blocks = g if fresh else past // bk

    def sweep(carries, accs, k_ref, v_ref, get, diag):
        out = [_sb_block(qs[h], get(k_ref, h), get(v_ref, h), carries[h], accs[h], diag=diag)
               for h in range(N_HEADS)]
        return tuple(c for c, _ in out), tuple(a for _, a in out)

    def alive(carries):
        m = functools.reduce(jnp.maximum, carries)
        return jnp.max(m) > SB_DEAD_LOG

    carries = tuple(jnp.zeros((bq, 1), F32) for _ in range(N_HEADS))
    accs = tuple(jnp.zeros((bq, HEAD_DIM), F32) for _ in range(N_HEADS))
    carries, accs = sweep(carries, accs, kd_ref, vd_ref, head_rows(bq), True)

    near = sweep(carries, accs, kp_ref, vp_ref, head_rows(bk), False)
    if fresh:
        has_past = g >= 1
        carries, accs = jax.tree.map(lambda new, old: jnp.where(has_past, new, old), near, (carries, accs))
    else:
        carries, accs = near

    def cond(st):
        c, carries, _ = st
        return jnp.logical_and(c < n_blocks, alive(carries))

    def body(st):
        c, carries, accs = st
        if fresh:
            start = pl.multiple_of((g - 1 - c) * bk * N_HEADS, bk * N_HEADS)
            k_src = kfar_ref.at[layer, pl.ds(start, bk * N_HEADS), :]
            v_src = vfar_ref.at[layer, pl.ds(start, bk * N_HEADS), :]
        else:
            start = pl.multiple_of((past - (c + 1) * bk) * N_HEADS, bk * N_HEADS)
            k_src = kfar_ref.at[layer, g, pl.ds(start, bk * N_HEADS), :]
            v_src = vfar_ref.at[layer, g, pl.ds(start, bk * N_HEADS), :]
        k_copy = pltpu.make_async_copy(k_src, kbuf_ref, sem_ref.at[0])
        v_copy = pltpu.make_async_copy(v_src, vbuf_ref, sem_ref.at[1])
        k_copy.start()
        v_copy.start()
        k_copy.wait()
        v_copy.wait()
        carries, accs = sweep(carries, accs, kbuf_ref, vbuf_ref, head_rows(bk), False)
        return c + 1, carries, accs

    _, _, accs = lax.while_loop(cond, body, (jnp.int32(1), carries, accs))
    for h in range(N_HEADS):
        o_ref[:, _hcols(h)] = accs[h].astype(BF16)


def stick_breaking(zq, k_new, v_new, k_past, v_past, layer, *, batch, seq):
    t = zq.shape[0]
    fresh = k_past is None
    if fresh:
        assert batch == 1
        bq = bk = min(ROW_GROUP, t)
        past = None
        near = pl.BlockSpec((None, bk * N_HEADS, HEAD_DIM), lambda g: (layer, jnp.maximum(g - 1, 0), 0))
        k_far, v_far = k_new, v_new
    else:
        bq = seq
        past = k_past.shape[2] // N_HEADS
        bk = min(ROW_GROUP, past)
        assert past % bk == 0
        last = past // bk - 1
        near = pl.BlockSpec((None, None, bk * N_HEADS, HEAD_DIM), lambda g: (layer, g, last, 0))
        k_far, v_far = k_past, v_past
    blk = (bq, GROUP_W)
    diag = pl.BlockSpec((None, bq * N_HEADS, HEAD_DIM), lambda g: (layer, g, 0))
    any_spec = pl.BlockSpec(memory_space=pl.ANY)
    buf = pltpu.VMEM((bk * N_HEADS, HEAD_DIM), F32)
    return pl.pallas_call(
        functools.partial(_sb_kernel, bq=bq, bk=bk, fresh=fresh, layer=layer, past=past),
        grid=(t // bq,),
        in_specs=[pl.BlockSpec(blk, lambda g: (g, 0)), diag, diag, near, near, any_spec, any_spec],
        out_specs=pl.BlockSpec(blk, lambda g: (g, 0)),
        out_shape=jax.ShapeDtypeStruct((t, GROUP_W), BF16),
        scratch_shapes=[buf, buf, pltpu.SemaphoreType.DMA((2,))],
        compiler_params=_cparams("arbitrary"),
        name="stick_breaking",
    )(zq, k_new, v_new, k_far, v_far, k_far, v_far)


def _out_proj_kernel(x_ref, m0_ref, m1_ref, m2_ref, m3_ref, w_ref, o_ref):
    acc = x_ref[...]
    for gi, m_ref in enumerate((m0_ref, m1_ref, m2_ref, m3_ref)):
        acc = acc + _dot(m_ref[...], w_ref[gi * GROUP_W:(gi + 1) * GROUP_W, :])
    o_ref[...] = acc


def out_proj(x, mixes, w, layer, *, tm=512):
    t, d = x.shape
    tm = min(tm, t)
    mix_spec = pl.BlockSpec((tm, GROUP_W), lambda i: (i, 0))
    return pl.pallas_call(
        _out_proj_kernel,
        grid=(t // tm,),
        in_specs=[pl.BlockSpec((tm, d), lambda i: (i, 0)), mix_spec, mix_spec, mix_spec, mix_spec,
                  pl.BlockSpec((None,) + w.shape[1:], lambda i: (layer, 0, 0), pipeline_mode=pl.Buffered(1))],
        out_specs=pl.BlockSpec((tm, d), lambda i: (i, 0)),
        out_shape=jax.ShapeDtypeStruct((t, d), F32),
        compiler_params=_cparams("parallel"),
        name="out_proj",
    )(x, *mixes, w)


def _mlp_kernel(x_ref, g_ref, wu_ref, wd_ref, *rest, final, n_cast):
    rest = list(rest)
    fg_ref = rest.pop(0) if final else None
    cast_in = [rest.pop(0) for _ in range(n_cast)]
    o_ref = rest.pop(0)
    cast_out = [rest.pop(0) for _ in range(n_cast)]
    (h_ref,) = rest
    f = pl.program_id(1)

    @pl.when(f == 0)
    def _():
        x = x_ref[...]
        h_ref[...] = _rms(x, g_ref[...]).astype(BF16)
        o_ref[...] = x

    u = _dot(h_ref[...], wu_ref[...])
    a = jnp.square(jnp.maximum(u, 0.0)).astype(BF16)
    o_ref[...] += _dot(a, wd_ref[...])

    for src_ref, dst_ref in zip(cast_in, cast_out):
        dst_ref[...] = src_ref[...].astype(BF16)

    if final:
        @pl.when(f == pl.num_programs(1) - 1)
        def _():
            o_ref[...] = _rms(o_ref[...], fg_ref[...])


def mlp(x, g, w_up, w_down, final_g=None, cast_next=None, *, tm=1024, tf=512):
    t, d = x.shape
    dff = w_up.shape[1]
    tm = min(tm, t)
    nf = dff // tf
    steps = (t // tm) * nf
    final = final_g is not None
    vec_spec = pl.BlockSpec((1, d), lambda i, f: (0, 0))
    in_specs = [pl.BlockSpec((tm, d), lambda i, f: (i, 0)), vec_spec,
                pl.BlockSpec((d, tf), lambda i, f: (0, f)), pl.BlockSpec((tf, d), lambda i, f: (f, 0))]
    args = [x, g.reshape(1, d), w_up, w_down]
    if final:
        in_specs.append(vec_spec)
        args.append(final_g.reshape(1, d))
    out_specs = [pl.BlockSpec((tm, d), lambda i, f: (i, 0))]
    out_shape = [jax.ShapeDtypeStruct((t, d), F32)]
    n_cast = 0
    if cast_next is not None:
        layer, stacked = cast_next
        n_cast = len(stacked)
        for w in stacked:
            _, r, c = w.shape
            slab = r // steps
            assert slab * steps == r and slab % 16 == 0, (w.shape, steps)
            in_specs.append(pl.BlockSpec((None, slab, c), lambda i, f: (layer, i * nf + f, 0)))
            args.append(w)
            out_specs.append(pl.BlockSpec((slab, c), lambda i, f: (i * nf + f, 0)))
            out_shape.append(jax.ShapeDtypeStruct((r, c), BF16))
    out = pl.pallas_call(
        functools.partial(_mlp_kernel, final=final, n_cast=n_cast),
        grid=(t // tm, nf),
        in_specs=in_specs,
        out_specs=out_specs,
        out_shape=out_shape,
        scratch_shapes=[pltpu.VMEM((tm, d), BF16)],
        compiler_params=_cparams("parallel", "arbitrary"),
        name="mlp",
    )(*args)
    return (out[0], out[1:]) if n_cast else out[0]


def _rope_tables(pos):
    half = HEAD_DIM // 2
    inv = ROPE_BASE ** (-jnp.arange(half, dtype=F32) / half)
    ang = pos.astype(F32)[:, None] * inv[None, :]
    cos = jnp.cos(ang)
    sin = jnp.sin(ang)
    return jnp.concatenate([cos, cos], axis=-1), jnp.concatenate([-sin, sin], axis=-1)


class _Group:
    def __init__(self, x, pos0, states, depth):
        self.batch, self.seq, self.d = x.shape
        self.pos0 = pos0
        t = self.batch * self.seq
        self.x2 = x.reshape(t, self.d)
        rows = min(ROW_GROUP, t)
        self.cos, self.sin = _rope_tables(pos0 + jnp.arange(self.seq, dtype=jnp.int32))
        if states is None:
            self.mix_states = self.k_past = self.v_past = None
        else:
            reps = rows // self.seq
            self.cos, self.sin = jnp.tile(self.cos, (reps, 1)), jnp.tile(self.sin, (reps, 1))
            past = states[1].shape[2]
            self.k_past = states[1].reshape(depth, self.batch, past * N_HEADS, HEAD_DIM)
            self.v_past = states[2].reshape(depth, self.batch, past * N_HEADS, HEAD_DIM)
            pool_s = jnp.pad(states[3], ((0, 0), (0, 0), (POOL_HEAD - POOL_PAD, 0), (0, 0)))
            self.mix_states = (states[0], pool_s, states[4])
        self.depth = depth
        self.kv_stack = None
        self.rets, self.pools, self.hgs = [], [], []

    def layer(self, l, p, lb, wb, final_g, cast_next):
        bs = dict(batch=self.batch, seq=self.seq)
        zq, ks, vs, ro, po, ho, ret_new, hg_new, pool_rows = in_proj_mix(
            self.x2, p["norm1_g"][l], wb[0], self.cos, self.sin, p, lb, l, self.mix_states, self.kv_stack,
            depth=self.depth, pos0=self.pos0, **bs)
        self.kv_stack = (ks, vs)
        so = stick_breaking(zq, ks, vs, self.k_past, self.v_past, l, **bs)
        x1 = out_proj(self.x2, (ro, so, po, ho), p["w_out"], l)
        out = mlp(x1, p["norm2_g"][l], wb[1], wb[2], final_g, cast_next)
        self.x2, casts = out if cast_next is not None else (out, None)
        state_shape = (self.batch, N_HEADS, HEAD_DIM, HEAD_DIM)
        self.rets.append(ret_new.reshape(state_shape))
        self.hgs.append(hg_new.reshape(state_shape))
        self.pools.append(pool_rows.reshape(self.batch, POOL_HEAD, GROUP_W)[:, POOL_HEAD - POOL_PAD:, :])
        return casts

    def outputs(self):
        ks, vs = self.kv_stack
        kv_shape = (self.depth, self.batch, self.seq, N_HEADS, HEAD_DIM)
        y = self.x2.reshape(self.batch, self.seq, self.d)
        return (y, jnp.stack(self.rets), ks.reshape(kv_shape), vs.reshape(kv_shape),
                jnp.stack(self.pools), jnp.stack(self.hgs))


def kernel(x_prompt, x_sample, state_ret, cache_sb_k, cache_sb_v, state_pool, state_hgrn, norm1_g, w_in, ret_norm_g, pool_w, pool_scale, hg_lower_bounds, hg_norm_g, w_out, norm2_g, w_up, w_down, final_norm_g):
    lb_all = jnp.cumsum(jax.nn.softmax(hg_lower_bounds.astype(F32), axis=0), axis=0)
    lb_all = lb_all - lb_all[0:1]
    depth = w_in.shape[0]
    params = dict(norm1_g=norm1_g, ret_norm_g=ret_norm_g, pool_w=pool_w.astype(BF16), pool_scale=pool_scale,
                  hg_norm_g=hg_norm_g, w_out=w_out.astype(BF16), norm2_g=norm2_g)
    big = [w_in, w_up, w_down]
    prompt = _Group(x_prompt, 0, None, depth)
    streams = _Group(x_sample, cache_sb_k.shape[2],
                     (state_ret, cache_sb_k, cache_sb_v, state_pool, state_hgrn), depth)
    wb = tuple(w[0].astype(BF16) for w in big)
    for l in range(depth):
        final_g = final_norm_g if l == depth - 1 else None
        cast_next = (l + 1, big) if l + 1 < depth else None
        wb_next = prompt.layer(l, params, lb_all[l], wb, final_g, cast_next)
        streams.layer(l, params, lb_all[l], wb, final_g, None)
        wb = wb_next
    y_p, ret_p, sbk_p, sbv_p, pool_p, hg_p = prompt.outputs()
    y_s, ret_s, sbk_s, sbv_s, pool_s, hg_s = streams.outputs()
    return (y_p, y_s, ret_p, ret_s, sbk_p, sbv_p, sbk_s, sbv_s, pool_p, pool_s, hg_p, hg_s)
```

```python
import functools
import math

import numpy as np
import jax
import jax.numpy as jnp
from jax import lax
from jax.experimental import pallas as pl
from jax.experimental.pallas import tpu as pltpu

F32 = jnp.float32
BF16 = jnp.bfloat16

HEAD_DIM = 128
N_HEADS = 4
GROUP_W = N_HEADS * HEAD_DIM
N_SLOTS = 12
SB_SLOT0, N_SB_SLOTS = 4, 3
POOL_WINDOWS = (2, 4, 8, 16)
POOL_PAD = max(POOL_WINDOWS) - 1
POOL_HEAD = 16
ROPE_BASE = 10000.0
EPS = 1e-6
SB_SCALE = HEAD_DIM ** -0.5
SB_DEAD_LOG = -104.0

VMEM_LIMIT = 60 * 1024 * 1024
ROW_GROUP = 256
HGRN_CHUNK = 64
HGRN_SUB = 128


def _cparams(*sem):
    return pltpu.CompilerParams(dimension_semantics=sem, vmem_limit_bytes=VMEM_LIMIT)


def _dot(a, b):
    return jnp.dot(a, b, preferred_element_type=F32)


def _dot_nt(a, b):
    return lax.dot_general(a, b, (((1,), (1,)), ((), ())), preferred_element_type=F32)


def _sigmoid(x):
    return 1.0 / (1.0 + jnp.exp(-x))


def _log_sigmoid(x):
    return jnp.minimum(x, 0.0) - jnp.log(1.0 + jnp.exp(-jnp.abs(x)))


def _rms(x, g):
    ms = jnp.mean(x * x, axis=-1, keepdims=True)
    return x * lax.rsqrt(ms + EPS) * g


def _hcols(h):
    return slice(h * HEAD_DIM, (h + 1) * HEAD_DIM)


def _slot_cols(slot, h):
    return slice(slot * GROUP_W + h * HEAD_DIM, slot * GROUP_W + (h + 1) * HEAD_DIM)


def _head_norm_gate(o, gn, gate):
    return (_rms(o, gn) * gate).astype(BF16)


def _retention_tables(lg, rows, chunk):
    row = lax.broadcasted_iota(jnp.int32, (rows, rows), 0)
    col = lax.broadcasted_iota(jnp.int32, (rows, rows), 1)
    diff = row - col
    ok = jnp.where((row ^ col) < chunk, diff, -1) >= 0
    dmask = jnp.where(ok, jnp.exp(jnp.maximum(diff, 0).astype(F32) * lg), 0.0)
    pos = (lax.broadcasted_iota(jnp.int32, (rows, HEAD_DIM), 0) & (chunk - 1)).astype(F32)
    return dmask, jnp.exp((pos + 1.0) * lg), jnp.exp((chunk - 1.0 - pos) * lg)


def _retention_head(q, k, v, gate, cos, sin, gn, dmask, qdec, kdec, state_in, chunk, nseq):
    def rope(x):
        return x * cos + pltpu.roll(x, HEAD_DIM // 2, 1) * sin

    q = rope(q)
    k = rope(k) * SB_SCALE
    vb = v.astype(BF16)
    scores = _dot_nt(q.astype(BF16), k.astype(BF16)) * dmask
    o_intra = _dot(scores.astype(BF16), vb)
    qd = (q * qdec).astype(BF16)
    kd = k * kdec
    cdec = qdec[chunk - 1:chunk, :]
    outs, states = [], []
    for b in range(nseq):
        sl = slice(b * chunk, (b + 1) * chunk)
        s_old = state_in(b)
        outs.append(o_intra[sl] + _dot(qd[sl], s_old.astype(BF16)))
        states.append(cdec * s_old + _dot(kd[sl].T.astype(BF16), vb[sl]))
    o = outs[0] if nseq == 1 else jnp.concatenate(outs, axis=0)
    return _head_norm_gate(o, gn, gate * _sigmoid(gate)), states


def _hgrn_tables(rows, chunk):
    nlev = int(math.log2(chunk))
    l = np.arange(rows)[:, None]
    m = np.arange(rows)[None, :]
    same = (l // chunk) == (m // chunk)
    x = np.bitwise_xor(l, m)
    hb = np.floor(np.log2(np.maximum(x, 1))).astype(np.int64)
    level = nlev - 1 - hb
    lv = np.where(same & (l > m), level, np.where(l == m, nlev, -1)).astype(np.float32)
    tri = (same & (l >= m)).astype(np.float32)
    return jnp.asarray(lv, BF16), jnp.asarray(np.tile(tri, (1, 3)), BF16)


def _split3(x):
    h1 = x.astype(BF16)
    r1 = x - h1.astype(F32)
    h2 = r1.astype(BF16)
    h3 = (r1 - h2.astype(F32)).astype(BF16)
    return h1, h2, h3


def _round_robin(gens):
    results = [None] * len(gens)
    live = list(enumerate(gens))
    while live:
        still = []
        for i, gen in live:
            try:
                next(gen)
                still.append((i, gen))
            except StopIteration as done:
                results[i] = done.value
        live = still
    return results


def _hgrn_head(gq, fl, vi, gate, lb, gn, lv, tri, gcum_ref, state_in, chunk, nseq, chained):
    rows = chunk * nseq
    nlev = int(math.log2(chunk))
    q = gq * _sigmoid(gq)
    log_lb, log1m_lb, one_m_lb = lb[0:1, :], lb[1:2, :], lb[2:3, :]
    t = jnp.exp(-jnp.abs(fl))
    r = 1.0 / (1.0 + t)
    sig_neg = jnp.where(fl >= 0.0, t * r, r)
    b = log1m_lb + (jnp.minimum(fl, 0.0) - jnp.log(1.0 + t))
    log_f = jnp.maximum(log_lb, b) + jnp.log(1.0 + jnp.exp(-jnp.abs(log_lb - b)))
    kk = one_m_lb * sig_neg
    vb = vi.astype(BF16)
    yield

    sub = lv.shape[0]
    subs = [slice(r0, r0 + sub) for r0 in range(0, rows, sub)]

    h1, h2, h3 = _split3(log_f)
    gcum = jnp.concatenate([_dot(tri, jnp.concatenate([h1[r], h2[r], h3[r]], axis=0)) for r in subs], axis=0)
    gcum_ref[...] = gcum
    yield

    ridx = lax.broadcasted_iota(jnp.int32, (rows, HEAD_DIM), 0)
    qb = q.astype(BF16)
    kb = kk.astype(BF16)
    zero_scores = jnp.zeros((sub, sub), BF16)
    scores = [jnp.where(lv == nlev, _dot_nt(qb[r], kb[r]).astype(BF16), zero_scores) for r in subs]

    def add_level(scores, level, e):
        w = jnp.exp(e).astype(BF16)
        qw = qb * w
        kw = kb * w
        return [jnp.where(lv == level, _dot_nt(qw[r], kw[r]).astype(BF16), sc) for r, sc in zip(subs, scores)]

    for level in range(nlev):
        blk = chunk >> level
        if blk < 16:
            break
        half = blk // 2
        pieces = [jnp.broadcast_to(gcum_ref[r0 + half - 1:r0 + half, :], (blk, HEAD_DIM))
                  for r0 in range(0, rows, blk)]
        gmid = pieces[0] if len(pieces) == 1 else jnp.concatenate(pieces, axis=0)
        scores = add_level(scores, level, -jnp.abs(gcum - gmid))
        yield

    f1 = pltpu.roll(log_f, 1, 0)
    f2 = pltpu.roll(log_f, 2, 0)
    f3 = pltpu.roll(log_f, 3, 0)
    b1 = pltpu.roll(log_f, rows - 1, 0)
    b2 = pltpu.roll(log_f, rows - 2, 0)
    b3 = pltpu.roll(log_f, rows - 3, 0)
    a1 = log_f
    a2 = a1 + f1
    a3 = a2 + f2
    a4 = a3 + f3
    c1 = b1
    c2 = c1 + b2
    c3 = c2 + b3
    zero = jnp.zeros_like(log_f)

    def pick(idx, table):
        out = table[-1]
        for j in range(len(table) - 2, -1, -1):
            out = jnp.where(idx == j, table[j], out)
        return out

    small = {8: pick(ridx & 7, [c3, c2, c1, zero, a1, a2, a3, a4]),
             4: pick(ridx & 3, [c1, zero, a1, a2]),
             2: pick(ridx & 1, [zero, a1])}
    for blk in (8, 4, 2):
        if blk <= chunk:
            scores = add_level(scores, nlev - int(math.log2(blk)), small[blk])
            yield

    o_intra = jnp.concatenate([_dot(sc, vb[r]) for r, sc in zip(subs, scores)], axis=0)
    qg = (q * jnp.exp(gcum)).astype(BF16)
    yield
    outs, states = [], []
    st_chain = state_in(0) if chained else None
    for bi in range(nseq):
        sl = slice(bi * chunk, (bi + 1) * chunk)
        glast = gcum_ref[(bi + 1) * chunk - 1:(bi + 1) * chunk, :]
        st_old = st_chain if chained else state_in(bi)
        outs.append(o_intra[sl] + _dot_nt(qg[sl], st_old.astype(BF16)))
        kdec = (kk[sl] * jnp.exp(glast - gcum[sl])).astype(BF16)
        st_new = st_old * jnp.exp(glast) + _dot(vi[sl].T.astype(BF16), kdec)
        if chained:
            st_chain = st_new
        else:
            states.append(st_new)
        yield
    if chained:
        states = [st_chain]
    o = outs[0] if nseq == 1 else jnp.concatenate(outs, axis=0)
    return _head_norm_gate(o, gn, _sigmoid(gate)), states


def _pool_group(ext_ref, pw_ref, psc_ref, o_ref, pos1, seq, nseq):
    for gi, w in enumerate(POOL_WINDOWS):
        cols = _hcols(gi)
        cnt = jnp.minimum(pos1, w).astype(F32)
        ps = []
        for b in range(nseq):
            tok = ext_ref[b, POOL_HEAD:POOL_HEAD + seq, cols]
            wsum = tok
            for i in range(1, w):
                wsum = wsum + ext_ref[b, POOL_HEAD - i:POOL_HEAD - i + seq, cols]
            ps.append(wsum / cnt - tok)
        p = ps[0] if nseq == 1 else jnp.concatenate(ps, axis=0)
        y = _dot(p.astype(BF16), pw_ref[gi]) * psc_ref[:, cols]
        o_ref[:, cols] = y.astype(BF16)


_KEEP_RET, _KEEP_POOL, _KEEP_HGRN = 0, 4, 5
_N_KEEP = 9


def _in_mix_kernel(lg_ref, x_ref, g1_ref, w_ref, cos_ref, sin_ref, rgn_ref, pw_ref, psc_ref, lb_ref, hgn_ref,
                   lv_ref, tri_ref, *rest, seq, nseq, carry, pos0, n_groups, kv_alias, layer):
    rest = list(rest)
    if carry:
        rs0_ref = ps0_ref = hs0_ref = None
    else:
        rs0_ref, ps0_ref, hs0_ref = rest[:3]
        del rest[:3]
    if kv_alias:
        del rest[:2]
    zq_ref, ko_ref, vo_ref, ro_ref, po_ref, ho_ref, rs_out, hs_out, pc_out = rest[:9]
    zr, dmask_ref, qdec_ref, kdec_ref, gcum_ref, ext_ref = rest[9:15]
    rstate_ref, hstate_ref = rest[15:] if carry else (None, None)
    rows = seq * nseq
    ret_chunk = seq
    hg_chunk = min(HGRN_CHUNK, seq)
    g = pl.program_id(0)

    @pl.when(g == 0)
    def _():
        for h in range(N_HEADS):
            dmask_ref[h], qdec_ref[h], kdec_ref[h] = _retention_tables(lg_ref[h], rows, ret_chunk)
        if carry:
            rstate_ref[...] = jnp.zeros_like(rstate_ref)
            hstate_ref[...] = jnp.zeros_like(hstate_ref)
            ext_ref[:, 0:POOL_HEAD, :] = jnp.zeros((nseq, POOL_HEAD, GROUP_W), F32)

    hx = _rms(x_ref[...], g1_ref[...]).astype(BF16)
    slots = iter(range(N_SLOTS))

    def project(n):
        for _ in range(n):
            s = next(slots)
            zs = _dot(hx, w_ref[:, s * GROUP_W:(s + 1) * GROUP_W])
            if s == SB_SLOT0:
                zq_ref[...] = zs
            elif SB_SLOT0 < s < SB_SLOT0 + N_SB_SLOTS:
                kv_ref = ko_ref if s == SB_SLOT0 + 1 else vo_ref
                if not kv_alias:
                    for other in range(kv_ref.shape[0]):
                        if other != layer:
                            kv_ref[other] = jnp.zeros(kv_ref.shape[1:], F32)
                    kv_ref = kv_ref.at[layer]
                for h in range(N_HEADS):
                    kv_ref[pl.ds(h, rows, stride=N_HEADS), :] = zs[:, _hcols(h)]
            else:
                k = s if s < SB_SLOT0 else s - N_SB_SLOTS
                zr[:, k * GROUP_W:(k + 1) * GROUP_W] = zs

    project(4)
    cos = cos_ref[...]
    sin = sin_ref[...]
    for h in range(N_HEADS):
        project(1)
        state_in = (lambda b, h=h: rstate_ref[h]) if carry else (lambda b, h=h: rs0_ref[b, h])
        o, states = _retention_head(
            zr[:, _slot_cols(_KEEP_RET + 0, h)], zr[:, _slot_cols(_KEEP_RET + 1, h)],
            zr[:, _slot_cols(_KEEP_RET + 2, h)], zr[:, _slot_cols(_KEEP_RET + 3, h)],
            cos, sin, rgn_ref[:, _hcols(h)], dmask_ref[h], qdec_ref[h], kdec_ref[h], state_in, ret_chunk, nseq)
        ro_ref[:, _hcols(h)] = o
        for b, st in enumerate(states):
            if carry:
                rstate_ref[h] = st
            else:
                rs_out[b, h] = st

    project(2)
    if carry:
        pos_base = g * rows + pos0
    else:
        pos_base = pos0
    for b in range(nseq):
        if not carry:
            ext_ref[b, 0:POOL_HEAD, :] = ps0_ref[b]
        ext_ref[b, POOL_HEAD:POOL_HEAD + seq, :] = zr[b * seq:(b + 1) * seq,
                                                      _KEEP_POOL * GROUP_W:(_KEEP_POOL + 1) * GROUP_W]
    pos1 = lax.broadcasted_iota(jnp.int32, (seq, HEAD_DIM), 0) + (pos_base + 1)
    _pool_group(ext_ref, pw_ref, psc_ref, po_ref, pos1, seq, nseq)
    for b in range(nseq):
        tail = ext_ref[b, seq:seq + POOL_HEAD, :]
        if carry:
            ext_ref[b, 0:POOL_HEAD, :] = tail
            pc_out[...] = tail
        else:
            pc_out[b] = tail

    project(2)
    lv = lv_ref[...]
    tri = tri_ref[...]
    hg_nseq = rows // hg_chunk
    heads = []
    for h in range(N_HEADS):
        state_in = (lambda b, h=h: hstate_ref[h]) if carry else (lambda b, h=h: hs0_ref[b, h].T)
        heads.append(_hgrn_head(
            zr[:, _slot_cols(_KEEP_HGRN + 0, h)], zr[:, _slot_cols(_KEEP_HGRN + 1, h)],
            zr[:, _slot_cols(_KEEP_HGRN + 2, h)], zr[:, _slot_cols(_KEEP_HGRN + 3, h)],
            lb_ref[:, _hcols(h)], hgn_ref[:, _hcols(h)], lv, tri, gcum_ref.at[h], state_in,
            hg_chunk, hg_nseq, chained=carry))
    for h, (o, states) in enumerate(_round_robin(heads)):
        ho_ref[:, _hcols(h)] = o
        for b, st in enumerate(states):
            if carry:
                hstate_ref[h] = st
            else:
                hs_out[b, h] = st.T

    assert next(slots, None) is None

    if carry:
        @pl.when(g == n_groups - 1)
        def _():
            rs_out[...] = rstate_ref[...]
            for h in range(N_HEADS):
                hs_out[h] = hstate_ref[h].T


def in_proj_mix(x, g1, w, cos, sin, p, lb, l, states, kv_stack, *, depth, batch, seq, pos0):
    t, d = x.shape
    carry = states is None
    rows = min(ROW_GROUP, t)
    if carry:
        assert batch == 1 and t % rows == 0
        kseq, nseq = rows, 1
    else:
        assert rows % seq == 0 and batch % (rows // seq) == 0
        kseq, nseq = seq, rows // seq
    n_groups = t // rows
    hg_chunk = min(HGRN_CHUNK, kseq)
    sub = min(HGRN_SUB, rows)
    assert sub % hg_chunk == 0 and rows % sub == 0
    lv, tri = _hgrn_tables(sub, hg_chunk)
    lg = jnp.asarray(np.log(1.0 - 2.0 ** (-5.0 - np.arange(N_HEADS))), F32)
    lb_tab = jnp.stack([jnp.log(lb), jnp.log1p(-lb), 1.0 - lb], axis=0)

    grp = lambda g: (g, 0)
    const = lambda g: (0, 0)
    tab_idx = grp if carry else const
    vec = pl.BlockSpec((1, GROUP_W), const)
    in_specs = [
        pl.BlockSpec(memory_space=pltpu.SMEM),
        pl.BlockSpec((rows, d), grp), pl.BlockSpec((1, d), const), pl.BlockSpec(w.shape, const),
        pl.BlockSpec((rows, HEAD_DIM), tab_idx), pl.BlockSpec((rows, HEAD_DIM), tab_idx), vec,
        pl.BlockSpec((None,) + p["pool_w"].shape[1:], lambda g: (l, 0, 0, 0)), vec,
        pl.BlockSpec((3, GROUP_W), const), vec,
        pl.BlockSpec((sub, sub), const), pl.BlockSpec((sub, 3 * sub), const),
    ]
    args = [lg, x, g1.reshape(1, d), w, cos, sin, p["ret_norm_g"][l].reshape(1, GROUP_W),
            p["pool_w"], p["pool_scale"][l].reshape(1, GROUP_W), lb_tab, p["hg_norm_g"][l].reshape(1, GROUP_W),
            lv, tri]
    mix_spec = pl.BlockSpec((rows, GROUP_W), grp)
    mix_shape = jax.ShapeDtypeStruct((t, GROUP_W), BF16)
    state_blk = (nseq, N_HEADS, HEAD_DIM, HEAD_DIM)
    if carry:
        whole = lambda shape: pl.BlockSpec(shape, lambda g: (0,) * len(shape))
        st_shape = (N_HEADS, HEAD_DIM, HEAD_DIM)
        state_specs = [whole(st_shape), whole(st_shape), whole((POOL_HEAD, GROUP_W))]
        state_shapes = [jax.ShapeDtypeStruct(st_shape, F32), jax.ShapeDtypeStruct(st_shape, F32),
                        jax.ShapeDtypeStruct((POOL_HEAD, GROUP_W), F32)]
    else:
        ret_s, pool_s, hg_s = states
        st_in = pl.BlockSpec((None,) + state_blk, lambda g: (l, g, 0, 0, 0))
        pool_in = pl.BlockSpec((None, nseq, POOL_HEAD, GROUP_W), lambda g: (l, g, 0, 0))
        in_specs += [st_in, pool_in, st_in]
        args += [ret_s, pool_s, hg_s]
        st_out = pl.BlockSpec(state_blk, lambda g: (g, 0, 0, 0))
        state_specs = [st_out, st_out, pl.BlockSpec((nseq, POOL_HEAD, GROUP_W), lambda g: (g, 0, 0))]
        state_shapes = [jax.ShapeDtypeStruct(ret_s.shape[1:], F32), jax.ShapeDtypeStruct(hg_s.shape[1:], F32),
                        jax.ShapeDtypeStruct((batch, POOL_HEAD, GROUP_W), F32)]
    scratch = [
        pltpu.VMEM((rows, _N_KEEP * GROUP_W), F32),
        pltpu.VMEM((N_HEADS, rows, rows), F32), pltpu.VMEM((N_HEADS, rows, HEAD_DIM), F32),
        pltpu.VMEM((N_HEADS, rows, HEAD_DIM), F32), pltpu.VMEM((N_HEADS, rows, HEAD_DIM), F32),
        pltpu.VMEM((nseq, POOL_HEAD + kseq, GROUP_W), F32),
    ]
    if carry:
        scratch += [pltpu.VMEM((N_HEADS, HEAD_DIM, HEAD_DIM), F32), pltpu.VMEM((N_HEADS, HEAD_DIM, HEAD_DIM), F32)]
    kv_alias = kv_stack is not None
    aliases = {}
    if kv_alias:
        aliases = {len(args): 1, len(args) + 1: 2}
        in_specs += [pl.BlockSpec(memory_space=pl.ANY)] * 2
        args += list(kv_stack)
    kv_shape = jax.ShapeDtypeStruct((depth, t * N_HEADS, HEAD_DIM), F32)
    if kv_alias:
        kv_spec = pl.BlockSpec((None, rows * N_HEADS, HEAD_DIM), lambda g: (l, g, 0))
    else:
        kv_spec = pl.BlockSpec((depth, rows * N_HEADS, HEAD_DIM), lambda g: (0, g, 0))
    return pl.pallas_call(
        functools.partial(_in_mix_kernel, seq=kseq, nseq=nseq, carry=carry, pos0=pos0, n_groups=n_groups,
                          kv_alias=kv_alias, layer=l),
        grid=(n_groups,),
        in_specs=in_specs,
        out_specs=[pl.BlockSpec((rows, GROUP_W), grp), kv_spec, kv_spec, mix_spec, mix_spec, mix_spec,
                   *state_specs],
        out_shape=[jax.ShapeDtypeStruct((t, GROUP_W), F32), kv_shape, kv_shape, mix_shape, mix_shape, mix_shape,
                   *state_shapes],
        scratch_shapes=scratch,
        input_output_aliases=aliases,
        compiler_params=_cparams("arbitrary"),
        name="in_proj_mix",
    )(*args)


def _strict_upper_ones_twice(n):
    row = lax.broadcasted_iota(jnp.int32, (2 * n, n), 0) & (n - 1)
    col = lax.broadcasted_iota(jnp.int32, (2 * n, n), 1)
    return jnp.where(row > col, 1.0, 0.0).astype(BF16)


def _sb_block(qb, kblk, vblk, state, *, diag):
    bq, bk = qb.shape[0], kblk.shape[0]
    nz = _dot_nt(qb, kblk.astype(BF16)) * (-SB_SCALE)
    yield
    lf = _log_sigmoid(nz)
    if diag:
        row = lax.broadcasted_iota(jnp.int32, (bq, bk), 0)
        col = lax.broadcasted_iota(jnp.int32, (bq, bk), 1)
        valid = col < row
        lf = jnp.where(valid, lf, 0.0)
    hi = lf.astype(BF16)
    lo = (lf - hi.astype(F32)).astype(BF16)
    later_in = _dot(jnp.concatenate([hi, lo], axis=1), _strict_upper_ones_twice(bk))
    yield
    carry, acc = state()
    a = jnp.exp(lf - nz + later_in + carry)
    if diag:
        a = jnp.where(valid, a, 0.0)
    acc = acc + _dot(a.astype(BF16), vblk.astype(BF16))
    carry = carry + later_in[:, 0:1] + lf[:, 0:1]
    return carry, acc


def _sb_kernel(q_ref, kd_ref, vd_ref, kp_ref, vp_ref, kfar_ref, vfar_ref, o_ref, kbuf_ref, vbuf_ref, sem_ref,
               *, bq, bk, fresh, layer, past):
    g = pl.program_id(0)
    qs = [q_ref[:, _hcols(h)].astype(BF16) for h in range(N_HEADS)]

    def head_rows(n):
        return lambda ref, h: ref[pl.ds(h, n, stride=N_HEADS), :]

    n_blocks = g if fresh else past // bk

    def block(h, k_ref, v_ref, get, state, diag):
        return _sb_block(qs[h], get(k_ref, h), get(v_ref, h), state, diag=diag)

    def keep(gen, holder, h):
        holder[h] = yield from gen

    def split(results):
        return tuple(c for c, _ in results), tuple(a for _, a in results)

    def alive(carries):
        m = functools.reduce(jnp.maximum, carries)
        return jnp.max(m) > SB_DEAD_LOG

    zero_state = (jnp.zeros((bq, 1), F32), jnp.zeros((bq, HEAD_DIM), F32))
    diag_res, near_res = [None] * N_HEADS, [None] * N_HEADS
    _round_robin(
        [keep(block(h, kd_ref, vd_ref, head_rows(bq), lambda: zero_state, True), diag_res, h)
         for h in range(N_HEADS)] +
        [keep(block(h, kp_ref, vp_ref, head_rows(bk), lambda h=h: diag_res[h], False), near_res, h)
         for h in range(N_HEADS)])
    carries, accs = split(near_res)
    if fresh:
        has_past = g >= 1
        carries, accs = jax.tree.map(lambda new, old: jnp.where(has_past, new, old),
                                     (carries, accs), split(diag_res))

    def cond(st):
        c, carries, _ = st
        return jnp.logical_and(c < n_blocks, alive(carries))

    def body(st):
        c, carries, accs = st
        if fresh:
            start = pl.multiple_of((g - 1 - c) * bk * N_HEADS, bk * N_HEADS)
            k_src = kfar_ref.at[layer, pl.ds(start, bk * N_HEADS), :]
            v_src = vfar_ref.at[layer, pl.ds(start, bk * N_HEADS), :]
        else:
            start = pl.multiple_of((past - (c + 1) * bk) * N_HEADS, bk * N_HEADS)
            k_src = kfar_ref.at[layer, g, pl.ds(start, bk * N_HEADS), :]
            v_src = vfar_ref.at[layer, g, pl.ds(start, bk * N_HEADS), :]
        k_copy = pltpu.make_async_copy(k_src, kbuf_ref, sem_ref.at[0])
        v_copy = pltpu.make_async_copy(v_src, vbuf_ref, sem_ref.at[1])
        k_copy.start()
        v_copy.start()
        k_copy.wait()
        v_copy.wait()
        carries, accs = split(_round_robin(
            [block(h, kbuf_ref, vbuf_ref, head_rows(bk), lambda h=h: (carries[h], accs[h]), False)
             for h in range(N_HEADS)]))
        return c + 1, carries, accs

    _, _, accs = lax.while_loop(cond, body, (jnp.int32(1), carries, accs))
    for h in range(N_HEADS):
        o_ref[:, _hcols(h)] = accs[h].astype(BF16)


def stick_breaking(zq, k_new, v_new, k_past, v_past, layer, *, batch, seq):
    t = zq.shape[0]
    fresh = k_past is None
    if fresh:
        assert batch == 1
        bq = bk = min(ROW_GROUP, t)
        past = None
        near = pl.BlockSpec((None, bk * N_HEADS, HEAD_DIM), lambda g: (layer, jnp.maximum(g - 1, 0), 0))
        k_far, v_far = k_new, v_new
    else:
        bq = seq
        past = k_past.shape[2] // N_HEADS
        bk = min(ROW_GROUP, past)
        assert past % bk == 0
        last = past // bk - 1
        near = pl.BlockSpec((None, None, bk * N_HEADS, HEAD_DIM), lambda g: (layer, g, last, 0))
        k_far, v_far = k_past, v_past
    blk = (bq, GROUP_W)
    diag = pl.BlockSpec((None, bq * N_HEADS, HEAD_DIM), lambda g: (layer, g, 0))
    any_spec = pl.BlockSpec(memory_space=pl.ANY)
    buf = pltpu.VMEM((bk * N_HEADS, HEAD_DIM), F32)
    return pl.pallas_call(
        functools.partial(_sb_kernel, bq=bq, bk=bk, fresh=fresh, layer=layer, past=past),
        grid=(t // bq,),
        in_specs=[pl.BlockSpec(blk, lambda g: (g, 0)), diag, diag, near, near, any_spec, any_spec],
        out_specs=pl.BlockSpec(blk, lambda g: (g, 0)),
        out_shape=jax.ShapeDtypeStruct((t, GROUP_W), BF16),
        scratch_shapes=[buf, buf, pltpu.SemaphoreType.DMA((2,))],
        compiler_params=_cparams("arbitrary"),
        name="stick_breaking",
    )(zq, k_new, v_new, k_far, v_far, k_far, v_far)


def _out_proj_kernel(x_ref, m0_ref, m1_ref, m2_ref, m3_ref, w_ref, o_ref):
    acc = x_ref[...]
    for gi, m_ref in enumerate((m0_ref, m1_ref, m2_ref, m3_ref)):
        acc = acc + _dot(m_ref[...], w_ref[gi * GROUP_W:(gi + 1) * GROUP_W, :])
    o_ref[...] = acc


def out_proj(x, mixes, w, *, tm=512):
    t, d = x.shape
    tm = min(tm, t)
    mix_spec = pl.BlockSpec((tm, GROUP_W), lambda i: (i, 0))
    return pl.pallas_call(
        _out_proj_kernel,
        grid=(t // tm,),
        in_specs=[pl.BlockSpec((tm, d), lambda i: (i, 0)), mix_spec, mix_spec, mix_spec, mix_spec,
                  pl.BlockSpec(w.shape, lambda i: (0, 0))],
        out_specs=pl.BlockSpec((tm, d), lambda i: (i, 0)),
        out_shape=jax.ShapeDtypeStruct((t, d), F32),
        compiler_params=_cparams("parallel"),
        name="out_proj",
    )(x, *mixes, w)


def _mlp_kernel(x_ref, g_ref, wu_ref, wd_ref, *rest, final, n_cast):
    rest = list(rest)
    fg_ref = rest.pop(0) if final else None
    cast_in = [rest.pop(0) for _ in range(n_cast)]
    o_ref = rest.pop(0)
    cast_out = [rest.pop(0) for _ in range(n_cast)]
    (h_ref,) = rest
    f = pl.program_id(1)

    @pl.when(f == 0)
    def _():
        x = x_ref[...]
        h_ref[...] = _rms(x, g_ref[...]).astype(BF16)
        o_ref[...] = x

    u = _dot(h_ref[...], wu_ref[...])
    a = jnp.square(jnp.maximum(u, 0.0)).astype(BF16)
    o_ref[...] += _dot(a, wd_ref[...])

    for src_ref, dst_ref in zip(cast_in, cast_out):
        dst_ref[...] = src_ref[...].astype(BF16)

    if final:
        @pl.when(f == pl.num_programs(1) - 1)
        def _():
            o_ref[...] = _rms(o_ref[...], fg_ref[...])


def mlp(x, g, w_up, w_down, final_g=None, cast_next=None, *, tm=1024, tf=512):
    t, d = x.shape
    dff = w_up.shape[1]
    tm = min(tm, t)
    nf = dff // tf
    steps = (t // tm) * nf
    final = final_g is not None
    vec_spec = pl.BlockSpec((1, d), lambda i, f: (0, 0))
    in_specs = [pl.BlockSpec((tm, d), lambda i, f: (i, 0)), vec_spec,
                pl.BlockSpec((d, tf), lambda i, f: (0, f)), pl.BlockSpec((tf, d), lambda i, f: (f, 0))]
    args = [x, g.reshape(1, d), w_up, w_down]
    if final:
        in_specs.append(vec_spec)
        args.append(final_g.reshape(1, d))
    out_specs = [pl.BlockSpec((tm, d), lambda i, f: (i, 0))]
    out_shape = [jax.ShapeDtypeStruct((t, d), F32)]
    n_cast = 0
    if cast_next is not None:
        layer, stacked = cast_next
        n_cast = len(stacked)
        for w in stacked:
            _, r, c = w.shape
            slab = r // steps
            assert slab * steps == r and slab % 16 == 0, (w.shape, steps)
            in_specs.append(pl.BlockSpec((None, slab, c), lambda i, f: (layer, i * nf + f, 0)))
            args.append(w)
            out_specs.append(pl.BlockSpec((slab, c), lambda i, f: (i * nf + f, 0)))
            out_shape.append(jax.ShapeDtypeStruct((r, c), BF16))
    out = pl.pallas_call(
        functools.partial(_mlp_kernel, final=final, n_cast=n_cast),
        grid=(t // tm, nf),
        in_specs=in_specs,
        out_specs=out_specs,
        out_shape=out_shape,
        scratch_shapes=[pltpu.VMEM((tm, d), BF16)],
        compiler_params=_cparams("parallel", "arbitrary"),
        name="mlp",
    )(*args)
    return (out[0], out[1:]) if n_cast else out[0]


def _rope_tables(pos):
    half = HEAD_DIM // 2
    inv = ROPE_BASE ** (-jnp.arange(half, dtype=F32) / half)
    ang = pos.astype(F32)[:, None] * inv[None, :]
    cos = jnp.cos(ang)
    sin = jnp.sin(ang)
    return jnp.concatenate([cos, cos], axis=-1), jnp.concatenate([-sin, sin], axis=-1)


class _Group:
    def __init__(self, x, pos0, states, depth):
        self.batch, self.seq, self.d = x.shape
        self.pos0 = pos0
        t = self.batch * self.seq
        self.x2 = x.reshape(t, self.d)
        rows = min(ROW_GROUP, t)
        self.cos, self.sin = _rope_tables(pos0 + jnp.arange(self.seq, dtype=jnp.int32))
        if states is None:
            self.mix_states = self.k_past = self.v_past = None
        else:
            reps = rows // self.seq
            self.cos, self.sin = jnp.tile(self.cos, (reps, 1)), jnp.tile(self.sin, (reps, 1))
            past = states[1].shape[2]
            self.k_past = states[1].reshape(depth, self.batch, past * N_HEADS, HEAD_DIM)
            self.v_past = states[2].reshape(depth, self.batch, past * N_HEADS, HEAD_DIM)
            pool_s = jnp.pad(states[3], ((0, 0), (0, 0), (POOL_HEAD - POOL_PAD, 0), (0, 0)))
            self.mix_states = (states[0], pool_s, states[4])
        self.depth = depth
        self.kv_stack = None
        self.rets, self.pools, self.hgs = [], [], []

    def layer(self, l, p, lb, wb, final_g, cast_next):
        bs = dict(batch=self.batch, seq=self.seq)
        zq, ks, vs, ro, po, ho, ret_new, hg_new, pool_rows = in_proj_mix(
            self.x2, p["norm1_g"][l], wb[0], self.cos, self.sin, p, lb, l, self.mix_states, self.kv_stack,
            depth=self.depth, pos0=self.pos0, **bs)
        self.kv_stack = (ks, vs)
        so = stick_breaking(zq, ks, vs, self.k_past, self.v_past, l, **bs)
        x1 = out_proj(self.x2, (ro, so, po, ho), wb[3])
        out = mlp(x1, p["norm2_g"][l], wb[1], wb[2], final_g, cast_next)
        self.x2, casts = out if cast_next is not None else (out, None)
        state_shape = (self.batch, N_HEADS, HEAD_DIM, HEAD_DIM)
        self.rets.append(ret_new.reshape(state_shape))
        self.hgs.append(hg_new.reshape(state_shape))
        self.pools.append(pool_rows.reshape(self.batch, POOL_HEAD, GROUP_W)[:, POOL_HEAD - POOL_PAD:, :])
        return casts

    def outputs(self):
        ks, vs = self.kv_stack
        kv_shape = (self.depth, self.batch, self.seq, N_HEADS, HEAD_DIM)
        y = self.x2.reshape(self.batch, self.seq, self.d)
        return (y, jnp.stack(self.rets), ks.reshape(kv_shape), vs.reshape(kv_shape),
                jnp.stack(self.pools), jnp.stack(self.hgs))


def kernel(x_prompt, x_sample, state_ret, cache_sb_k, cache_sb_v, state_pool, state_hgrn, norm1_g, w_in, ret_norm_g, pool_w, pool_scale, hg_lower_bounds, hg_norm_g, w_out, norm2_g, w_up, w_down, final_norm_g):
    lb_all = jnp.cumsum(jax.nn.softmax(hg_lower_bounds.astype(F32), axis=0), axis=0)
    lb_all = lb_all - lb_all[0:1]
    depth = w_in.shape[0]
    params = dict(norm1_g=norm1_g, ret_norm_g=ret_norm_g, pool_w=pool_w.astype(BF16), pool_scale=pool_scale,
                  hg_norm_g=hg_norm_g, norm2_g=norm2_g)
    big = [w_in, w_up, w_down, w_out]
    prompt = _Group(x_prompt, 0, None, depth)
    streams = _Group(x_sample, cache_sb_k.shape[2],
                     (state_ret, cache_sb_k, cache_sb_v, state_pool, state_hgrn), depth)
    wb = tuple(w[0].astype(BF16) for w in big)
    for l in range(depth):
        final_g = final_norm_g if l == depth - 1 else None
        cast_next = (l + 1, big) if l + 1 < depth else None
        wb_next = prompt.layer(l, params, lb_all[l], wb, final_g, cast_next)
        streams.layer(l, params, lb_all[l], wb, final_g, None)
        wb = wb_next
    y_p, ret_p, sbk_p, sbv_p, pool_p, hg_p = prompt.outputs()
    y_s, ret_s, sbk_s, sbv_s, pool_s, hg_s = streams.outputs()
    return (y_p, y_s, ret_p, ret_s, sbk_p, sbv_p, sbk_s, sbv_s, pool_p, pool_s, hg_p, hg_s)
```

```python
import functools
import math

import numpy as np
import jax
import jax.numpy as jnp
from jax import lax
from jax.experimental import pallas as pl
from jax.experimental.pallas import tpu as pltpu

F32 = jnp.float32
BF16 = jnp.bfloat16

HEAD_DIM = 128
BF16_SUBLANES = 16
N_HEADS = 4
GROUP_W = N_HEADS * HEAD_DIM
N_SLOTS = 12
SB_SLOT0, N_SB_SLOTS = 4, 3
POOL_WINDOWS = (2, 4, 8, 16)
POOL_PAD = max(POOL_WINDOWS) - 1
POOL_HEAD = 16
ROPE_BASE = 10000.0
EPS = 1e-6
SB_SCALE = HEAD_DIM ** -0.5
SB_DEAD_LOG = -104.0

VMEM_LIMIT = 60 * 1024 * 1024
ROW_GROUP = 256
HGRN_CHUNK = 64
HGRN_SUB = 128


def _cparams(*sem):
    return pltpu.CompilerParams(dimension_semantics=sem, vmem_limit_bytes=VMEM_LIMIT)


def _dot(a, b):
    return jnp.dot(a, b, preferred_element_type=F32)


def _dot_nt(a, b):
    return lax.dot_general(a, b, (((1,), (1,)), ((), ())), preferred_element_type=F32)


def _sigmoid(x):
    return 1.0 / (1.0 + jnp.exp(-x))


def _log_sigmoid(x):
    return jnp.minimum(x, 0.0) - jnp.log(1.0 + jnp.exp(-jnp.abs(x)))


def _rms(x, g):
    ms = jnp.mean(x * x, axis=-1, keepdims=True)
    return x * lax.rsqrt(ms + EPS) * g


def _hcols(h):
    return slice(h * HEAD_DIM, (h + 1) * HEAD_DIM)


def _slot_cols(slot, h):
    return slice(slot * GROUP_W + h * HEAD_DIM, slot * GROUP_W + (h + 1) * HEAD_DIM)


def _head_norm_gate(o, gn, gate):
    return (_rms(o, gn) * gate).astype(BF16)


def _retention_tables(lg, rows, chunk):
    row = lax.broadcasted_iota(jnp.int32, (rows, rows), 0)
    col = lax.broadcasted_iota(jnp.int32, (rows, rows), 1)
    diff = row - col
    ok = jnp.where((row ^ col) < chunk, diff, -1) >= 0
    dmask = jnp.where(ok, jnp.exp(jnp.maximum(diff, 0).astype(F32) * lg), 0.0)
    pos = (lax.broadcasted_iota(jnp.int32, (rows, HEAD_DIM), 0) & (chunk - 1)).astype(F32)
    return dmask, jnp.exp((pos + 1.0) * lg), jnp.exp((chunk - 1.0 - pos) * lg)


def _retention_head(q, k, v, gate, cos, sin, gn, dmask, qdec, kdec, state_in, chunk, nseq):
    def rope(x):
        return x * cos + pltpu.roll(x, HEAD_DIM // 2, 1) * sin

    q = rope(q)
    k = rope(k) * SB_SCALE
    vb = v.astype(BF16)
    scores = _dot_nt(q.astype(BF16), k.astype(BF16)) * dmask
    o_intra = _dot(scores.astype(BF16), vb)
    qd = (q * qdec).astype(BF16)
    kd = k * kdec
    cdec = qdec[chunk - 1:chunk, :]
    outs, states = [], []
    for b in range(nseq):
        sl = slice(b * chunk, (b + 1) * chunk)
        s_old = state_in(b)
        outs.append(o_intra[sl] + _dot(qd[sl], s_old.astype(BF16)))
        states.append(cdec * s_old + _dot(kd[sl].T.astype(BF16), vb[sl]))
    o = outs[0] if nseq == 1 else jnp.concatenate(outs, axis=0)
    return _head_norm_gate(o, gn, gate * _sigmoid(gate)), states


def _hgrn_tables(rows, chunk):
    nlev = int(math.log2(chunk))
    l = np.arange(rows)[:, None]
    m = np.arange(rows)[None, :]
    same = (l // chunk) == (m // chunk)
    x = np.bitwise_xor(l, m)
    hb = np.floor(np.log2(np.maximum(x, 1))).astype(np.int64)
    level = nlev - 1 - hb
    lv = np.where(same & (l > m), level, np.where(l == m, nlev, -1)).astype(np.float32)
    tri = (same & (l >= m)).astype(np.float32)
    return jnp.asarray(lv, BF16), jnp.asarray(np.tile(tri, (1, 3)), BF16)


def _split3(x):
    h1 = x.astype(BF16)
    r1 = x - h1.astype(F32)
    h2 = r1.astype(BF16)
    h3 = (r1 - h2.astype(F32)).astype(BF16)
    return h1, h2, h3


def _round_robin(gens):
    results = [None] * len(gens)
    live = list(enumerate(gens))
    while live:
        still = []
        for i, gen in live:
            try:
                next(gen)
                still.append((i, gen))
            except StopIteration as done:
                results[i] = done.value
        live = still
    return results


def _hgrn_head(gq, fl, vi, gate, lb, gn, lv, tri, gcum_ref, state_in, chunk, nseq, chained):
    rows = chunk * nseq
    nlev = int(math.log2(chunk))
    q = gq * _sigmoid(gq)
    log_lb, log1m_lb, one_m_lb = lb[0:1, :], lb[1:2, :], lb[2:3, :]
    t = jnp.exp(-jnp.abs(fl))
    r = 1.0 / (1.0 + t)
    sig_neg = jnp.where(fl >= 0.0, t * r, r)
    b = log1m_lb + (jnp.minimum(fl, 0.0) - jnp.log(1.0 + t))
    log_f = jnp.maximum(log_lb, b) + jnp.log(1.0 + jnp.exp(-jnp.abs(log_lb - b)))
    kk = one_m_lb * sig_neg
    vb = vi.astype(BF16)
    yield

    sub = lv.shape[0]
    subs = [slice(r0, r0 + sub) for r0 in range(0, rows, sub)]

    h1, h2, h3 = _split3(log_f)
    gcum = jnp.concatenate([_dot(tri, jnp.concatenate([h1[r], h2[r], h3[r]], axis=0)) for r in subs], axis=0)
    gcum_ref[...] = gcum
    yield

    ridx = lax.broadcasted_iota(jnp.int32, (rows, HEAD_DIM), 0)
    qb = q.astype(BF16)
    kb = kk.astype(BF16)
    zero_scores = jnp.zeros((sub, sub), BF16)
    scores = [jnp.where(lv == nlev, _dot_nt(qb[r], kb[r]).astype(BF16), zero_scores) for r in subs]

    def add_level(scores, level, e):
        w = jnp.exp(e).astype(BF16)
        qw = qb * w
        kw = kb * w
        return [jnp.where(lv == level, _dot_nt(qw[r], kw[r]).astype(BF16), sc) for r, sc in zip(subs, scores)]

    for level in range(nlev):
        blk = chunk >> level
        if blk < 16:
            break
        half = blk // 2
        pieces = [jnp.broadcast_to(gcum_ref[r0 + half - 1:r0 + half, :], (blk, HEAD_DIM))
                  for r0 in range(0, rows, blk)]
        gmid = pieces[0] if len(pieces) == 1 else jnp.concatenate(pieces, axis=0)
        scores = add_level(scores, level, -jnp.abs(gcum - gmid))
        yield

    f1 = pltpu.roll(log_f, 1, 0)
    f2 = pltpu.roll(log_f, 2, 0)
    f3 = pltpu.roll(log_f, 3, 0)
    b1 = pltpu.roll(log_f, rows - 1, 0)
    b2 = pltpu.roll(log_f, rows - 2, 0)
    b3 = pltpu.roll(log_f, rows - 3, 0)
    a1 = log_f
    a2 = a1 + f1
    a3 = a2 + f2
    a4 = a3 + f3
    c1 = b1
    c2 = c1 + b2
    c3 = c2 + b3
    zero = jnp.zeros_like(log_f)

    def pick(idx, table):
        out = table[-1]
        for j in range(len(table) - 2, -1, -1):
            out = jnp.where(idx == j, table[j], out)
        return out

    small = {8: pick(ridx & 7, [c3, c2, c1, zero, a1, a2, a3, a4]),
             4: pick(ridx & 3, [c1, zero, a1, a2]),
             2: pick(ridx & 1, [zero, a1])}
    for blk in (8, 4, 2):
        if blk <= chunk:
            scores = add_level(scores, nlev - int(math.log2(blk)), small[blk])
            yield

    o_intra = jnp.concatenate([_dot(sc, vb[r]) for r, sc in zip(subs, scores)], axis=0)
    qg = (q * jnp.exp(gcum)).astype(BF16)
    yield
    outs, states = [], []
    st_chain = state_in(0) if chained else None
    for bi in range(nseq):
        sl = slice(bi * chunk, (bi + 1) * chunk)
        glast = gcum_ref[(bi + 1) * chunk - 1:(bi + 1) * chunk, :]
        st_old = st_chain if chained else state_in(bi)
        outs.append(o_intra[sl] + _dot_nt(qg[sl], st_old.astype(BF16)))
        kdec = (kk[sl] * jnp.exp(glast - gcum[sl])).astype(BF16)
        st_new = st_old * jnp.exp(glast) + _dot(vi[sl].T.astype(BF16), kdec)
        if chained:
            st_chain = st_new
        else:
            states.append(st_new)
        yield
    if chained:
        states = [st_chain]
    o = outs[0] if nseq == 1 else jnp.concatenate(outs, axis=0)
    return _head_norm_gate(o, gn, _sigmoid(gate)), states


def _pool_group(ext_ref, pw_ref, psc_ref, o_ref, pos1, seq, nseq):
    for gi, w in enumerate(POOL_WINDOWS):
        cols = _hcols(gi)
        cnt = jnp.minimum(pos1, w).astype(F32)
        ps = []
        for b in range(nseq):
            tok = ext_ref[b, POOL_HEAD:POOL_HEAD + seq, cols]
            wsum = tok
            for i in range(1, w):
                wsum = wsum + ext_ref[b, POOL_HEAD - i:POOL_HEAD - i + seq, cols]
            ps.append(wsum / cnt - tok)
        p = ps[0] if nseq == 1 else jnp.concatenate(ps, axis=0)
        y = _dot(p.astype(BF16), pw_ref[gi]) * psc_ref[:, cols]
        o_ref[:, cols] = y.astype(BF16)


_KEEP_RET, _KEEP_POOL, _KEEP_HGRN = 0, 4, 5
_N_KEEP = 9


def _in_mix_kernel(lg_ref, x_ref, g1_ref, w_ref, cos_ref, sin_ref, rgn_ref, pw_ref, psc_ref, lb_ref, hgn_ref,
                   lv_ref, tri_ref, *rest, seq, nseq, carry, pos0, n_groups, kv_alias, layer):
    rest = list(rest)
    if carry:
        rs0_ref = ps0_ref = hs0_ref = None
    else:
        rs0_ref, ps0_ref, hs0_ref = rest[:3]
        del rest[:3]
    if kv_alias:
        del rest[:2]
    zq_ref, ko_ref, vo_ref, ro_ref, po_ref, ho_ref, rs_out, hs_out, pc_out = rest[:9]
    zr, dmask_ref, qdec_ref, kdec_ref, gcum_ref, ext_ref = rest[9:15]
    rstate_ref, hstate_ref = rest[15:] if carry else (None, None)
    rows = seq * nseq
    ret_chunk = seq
    hg_chunk = min(HGRN_CHUNK, seq)
    g = pl.program_id(0)

    @pl.when(g == 0)
    def _():
        for h in range(N_HEADS):
            dmask_ref[h], qdec_ref[h], kdec_ref[h] = _retention_tables(lg_ref[h], rows, ret_chunk)
        if carry:
            rstate_ref[...] = jnp.zeros_like(rstate_ref)
            hstate_ref[...] = jnp.zeros_like(hstate_ref)
            ext_ref[:, 0:POOL_HEAD, :] = jnp.zeros((nseq, POOL_HEAD, GROUP_W), F32)

    hx = _rms(x_ref[...], g1_ref[...]).astype(BF16)
    slots = iter(range(N_SLOTS))

    def project(n):
        for _ in range(n):
            s = next(slots)
            zs = _dot(hx, w_ref[:, s * GROUP_W:(s + 1) * GROUP_W])
            if s == SB_SLOT0:
                zq_ref[...] = zs
            elif SB_SLOT0 < s < SB_SLOT0 + N_SB_SLOTS:
                kv_ref = ko_ref if s == SB_SLOT0 + 1 else vo_ref
                if not kv_alias:
                    for other in range(kv_ref.shape[0]):
                        if other != layer:
                            kv_ref[other] = jnp.zeros(kv_ref.shape[1:], F32)
                    kv_ref = kv_ref.at[layer]
                for h in range(N_HEADS):
                    kv_ref[pl.ds(h, rows, stride=N_HEADS), :] = zs[:, _hcols(h)]
            else:
                k = s if s < SB_SLOT0 else s - N_SB_SLOTS
                zr[:, k * GROUP_W:(k + 1) * GROUP_W] = zs

    project(4)
    cos = cos_ref[...]
    sin = sin_ref[...]
    for h in range(N_HEADS):
        project(1)
        state_in = (lambda b, h=h: rstate_ref[h]) if carry else (lambda b, h=h: rs0_ref[b, h])
        o, states = _retention_head(
            zr[:, _slot_cols(_KEEP_RET + 0, h)], zr[:, _slot_cols(_KEEP_RET + 1, h)],
            zr[:, _slot_cols(_KEEP_RET + 2, h)], zr[:, _slot_cols(_KEEP_RET + 3, h)],
            cos, sin, rgn_ref[:, _hcols(h)], dmask_ref[h], qdec_ref[h], kdec_ref[h], state_in, ret_chunk, nseq)
        ro_ref[:, _hcols(h)] = o
        for b, st in enumerate(states):
            if carry:
                rstate_ref[h] = st
            else:
                rs_out[b, h] = st

    project(2)
    if carry:
        pos_base = g * rows + pos0
    else:
        pos_base = pos0
    for b in range(nseq):
        if not carry:
            ext_ref[b, 0:POOL_HEAD, :] = ps0_ref[b]
        ext_ref[b, POOL_HEAD:POOL_HEAD + seq, :] = zr[b * seq:(b + 1) * seq,
                                                      _KEEP_POOL * GROUP_W:(_KEEP_POOL + 1) * GROUP_W]
    pos1 = lax.broadcasted_iota(jnp.int32, (seq, HEAD_DIM), 0) + (pos_base + 1)
    _pool_group(ext_ref, pw_ref, psc_ref, po_ref, pos1, seq, nseq)
    for b in range(nseq):
        tail = ext_ref[b, seq:seq + POOL_HEAD, :]
        if carry:
            ext_ref[b, 0:POOL_HEAD, :] = tail
            pc_out[...] = tail
        else:
            pc_out[b] = tail

    project(2)
    lv = lv_ref[...]
    tri = tri_ref[...]
    hg_nseq = rows // hg_chunk
    heads = []
    for h in range(N_HEADS):
        state_in = (lambda b, h=h: hstate_ref[h]) if carry else (lambda b, h=h: hs0_ref[b, h].T)
        heads.append(_hgrn_head(
            zr[:, _slot_cols(_KEEP_HGRN + 0, h)], zr[:, _slot_cols(_KEEP_HGRN + 1, h)],
            zr[:, _slot_cols(_KEEP_HGRN + 2, h)], zr[:, _slot_cols(_KEEP_HGRN + 3, h)],
            lb_ref[:, _hcols(h)], hgn_ref[:, _hcols(h)], lv, tri, gcum_ref.at[h], state_in,
            hg_chunk, hg_nseq, chained=carry))
    for h, (o, states) in enumerate(_round_robin(heads)):
        ho_ref[:, _hcols(h)] = o
        for b, st in enumerate(states):
            if carry:
                hstate_ref[h] = st
            else:
                hs_out[b, h] = st.T

    assert next(slots, None) is None

    if carry:
        @pl.when(g == n_groups - 1)
        def _():
            rs_out[...] = rstate_ref[...]
            for h in range(N_HEADS):
                hs_out[h] = hstate_ref[h].T


def in_proj_mix(x, g1, w, cos, sin, p, lb, l, states, kv_stack, *, depth, batch, seq, pos0):
    t, d = x.shape
    carry = states is None
    rows = min(ROW_GROUP, t)
    if carry:
        assert batch == 1 and t % rows == 0
        kseq, nseq = rows, 1
    else:
        assert rows % seq == 0 and batch % (rows // seq) == 0
        kseq, nseq = seq, rows // seq
    n_groups = t // rows
    hg_chunk = min(HGRN_CHUNK, kseq)
    sub = min(HGRN_SUB, rows)
    assert sub % hg_chunk == 0 and rows % sub == 0
    lv, tri = _hgrn_tables(sub, hg_chunk)
    lg = jnp.asarray(np.log(1.0 - 2.0 ** (-5.0 - np.arange(N_HEADS))), F32)
    lb_tab = jnp.stack([jnp.log(lb), jnp.log1p(-lb), 1.0 - lb], axis=0)

    grp = lambda g: (g, 0)
    const = lambda g: (0, 0)
    tab_idx = grp if carry else const
    vec = pl.BlockSpec((1, GROUP_W), const)
    in_specs = [
        pl.BlockSpec(memory_space=pltpu.SMEM),
        pl.BlockSpec((rows, d), grp), pl.BlockSpec((1, d), const), pl.BlockSpec(w.shape, const),
        pl.BlockSpec((rows, HEAD_DIM), tab_idx), pl.BlockSpec((rows, HEAD_DIM), tab_idx), vec,
        pl.BlockSpec((None,) + p["pool_w"].shape[1:], lambda g: (l, 0, 0, 0)), vec,
        pl.BlockSpec((3, GROUP_W), const), vec,
        pl.BlockSpec((sub, sub), const), pl.BlockSpec((sub, 3 * sub), const),
    ]
    args = [lg, x, g1.reshape(1, d), w, cos, sin, p["ret_norm_g"][l].reshape(1, GROUP_W),
            p["pool_w"], p["pool_scale"][l].reshape(1, GROUP_W), lb_tab, p["hg_norm_g"][l].reshape(1, GROUP_W),
            lv, tri]
    mix_spec = pl.BlockSpec((rows, GROUP_W), grp)
    mix_shape = jax.ShapeDtypeStruct((t, GROUP_W), BF16)
    state_blk = (nseq, N_HEADS, HEAD_DIM, HEAD_DIM)
    if carry:
        whole = lambda shape: pl.BlockSpec(shape, lambda g: (0,) * len(shape))
        st_shape = (N_HEADS, HEAD_DIM, HEAD_DIM)
        state_specs = [whole(st_shape), whole(st_shape), whole((POOL_HEAD, GROUP_W))]
        state_shapes = [jax.ShapeDtypeStruct(st_shape, F32), jax.ShapeDtypeStruct(st_shape, F32),
                        jax.ShapeDtypeStruct((POOL_HEAD, GROUP_W), F32)]
    else:
        ret_s, pool_s, hg_s = states
        st_in = pl.BlockSpec((None,) + state_blk, lambda g: (l, g, 0, 0, 0))
        pool_in = pl.BlockSpec((None, nseq, POOL_HEAD, GROUP_W), lambda g: (l, g, 0, 0))
        in_specs += [st_in, pool_in, st_in]
        args += [ret_s, pool_s, hg_s]
        st_out = pl.BlockSpec(state_blk, lambda g: (g, 0, 0, 0))
        state_specs = [st_out, st_out, pl.BlockSpec((nseq, POOL_HEAD, GROUP_W), lambda g: (g, 0, 0))]
        state_shapes = [jax.ShapeDtypeStruct(ret_s.shape[1:], F32), jax.ShapeDtypeStruct(hg_s.shape[1:], F32),
                        jax.ShapeDtypeStruct((batch, POOL_HEAD, GROUP_W), F32)]
    scratch = [
        pltpu.VMEM((rows, _N_KEEP * GROUP_W), F32),
        pltpu.VMEM((N_HEADS, rows, rows), F32), pltpu.VMEM((N_HEADS, rows, HEAD_DIM), F32),
        pltpu.VMEM((N_HEADS, rows, HEAD_DIM), F32), pltpu.VMEM((N_HEADS, rows, HEAD_DIM), F32),
        pltpu.VMEM((nseq, POOL_HEAD + kseq, GROUP_W), F32),
    ]
    if carry:
        scratch += [pltpu.VMEM((N_HEADS, HEAD_DIM, HEAD_DIM), F32), pltpu.VMEM((N_HEADS, HEAD_DIM, HEAD_DIM), F32)]
    kv_alias = kv_stack is not None
    aliases = {}
    if kv_alias:
        aliases = {len(args): 1, len(args) + 1: 2}
        in_specs += [pl.BlockSpec(memory_space=pl.ANY)] * 2
        args += list(kv_stack)
    kv_shape = jax.ShapeDtypeStruct((depth, t * N_HEADS, HEAD_DIM), F32)
    if kv_alias:
        kv_spec = pl.BlockSpec((None, rows * N_HEADS, HEAD_DIM), lambda g: (l, g, 0))
    else:
        kv_spec = pl.BlockSpec((depth, rows * N_HEADS, HEAD_DIM), lambda g: (0, g, 0))
    return pl.pallas_call(
        functools.partial(_in_mix_kernel, seq=kseq, nseq=nseq, carry=carry, pos0=pos0, n_groups=n_groups,
                          kv_alias=kv_alias, layer=l),
        grid=(n_groups,),
        in_specs=in_specs,
        out_specs=[pl.BlockSpec((rows, GROUP_W), grp), kv_spec, kv_spec, mix_spec, mix_spec, mix_spec,
                   *state_specs],
        out_shape=[jax.ShapeDtypeStruct((t, GROUP_W), F32), kv_shape, kv_shape, mix_shape, mix_shape, mix_shape,
                   *state_shapes],
        scratch_shapes=scratch,
        input_output_aliases=aliases,
        compiler_params=_cparams("arbitrary"),
        name="in_proj_mix",
    )(*args)


def _strict_upper_ones_twice(n):
    row = lax.broadcasted_iota(jnp.int32, (2 * n, n), 0) & (n - 1)
    col = lax.broadcasted_iota(jnp.int32, (2 * n, n), 1)
    return jnp.where(row > col, 1.0, 0.0).astype(BF16)


def _sb_block(qb, kblk, vblk, state, *, diag):
    bq, bk = qb.shape[0], kblk.shape[0]
    nz = _dot_nt(qb, kblk.astype(BF16)) * (-SB_SCALE)
    yield
    lf = _log_sigmoid(nz)
    if diag:
        row = lax.broadcasted_iota(jnp.int32, (bq, bk), 0)
        col = lax.broadcasted_iota(jnp.int32, (bq, bk), 1)
        valid = col < row
        lf = jnp.where(valid, lf, 0.0)
    hi = lf.astype(BF16)
    lo = (lf - hi.astype(F32)).astype(BF16)
    later_in = _dot(jnp.concatenate([hi, lo], axis=1), _strict_upper_ones_twice(bk))
    yield
    carry, acc = state()
    a = jnp.exp(lf - nz + later_in + carry)
    if diag:
        a = jnp.where(valid, a, 0.0)
    acc = acc + _dot(a.astype(BF16), vblk.astype(BF16))
    carry = carry + later_in[:, 0:1] + lf[:, 0:1]
    return carry, acc


def _sb_kernel(q_ref, kd_ref, vd_ref, kp_ref, vp_ref, kfar_ref, vfar_ref, o_ref, kbuf_ref, vbuf_ref, sem_ref,
               *, bq, bk, fresh, layer, past):
    g = pl.program_id(0)
    qs = [q_ref[:, _hcols(h)].astype(BF16) for h in range(N_HEADS)]

    def head_rows(n):
        return lambda ref, h: ref[pl.ds(h, n, stride=N_HEADS), :]

    n_blocks = g if fresh else past // bk

    def block(h, k_ref, v_ref, get, state, diag):
        return _sb_block(qs[h], get(k_ref, h), get(v_ref, h), state, diag=diag)

    def keep(gen, holder, h):
        holder[h] = yield from gen

    def split(results):
        return tuple(c for c, _ in results), tuple(a for _, a in results)

    def alive(carries):
        m = functools.reduce(jnp.maximum, carries)
        return jnp.max(m) > SB_DEAD_LOG

    zero_state = (jnp.zeros((bq, 1), F32), jnp.zeros((bq, HEAD_DIM), F32))
    diag_res, near_res = [None] * N_HEADS, [None] * N_HEADS
    _round_robin(
        [keep(block(h, kd_ref, vd_ref, head_rows(bq), lambda: zero_state, True), diag_res, h)
         for h in range(N_HEADS)] +
        [keep(block(h, kp_ref, vp_ref, head_rows(bk), lambda h=h: diag_res[h], False), near_res, h)
         for h in range(N_HEADS)])
    carries, accs = split(near_res)
    if fresh:
        has_past = g >= 1
        carries, accs = jax.tree.map(lambda new, old: jnp.where(has_past, new, old),
                                     (carries, accs), split(diag_res))

    def cond(st):
        c, carries, _ = st
        return jnp.logical_and(c < n_blocks, alive(carries))

    def body(st):
        c, carries, accs = st
        if fresh:
            start = pl.multiple_of((g - 1 - c) * bk * N_HEADS, bk * N_HEADS)
            k_src = kfar_ref.at[layer, pl.ds(start, bk * N_HEADS), :]
            v_src = vfar_ref.at[layer, pl.ds(start, bk * N_HEADS), :]
        else:
            start = pl.multiple_of((past - (c + 1) * bk) * N_HEADS, bk * N_HEADS)
            k_src = kfar_ref.at[layer, g, pl.ds(start, bk * N_HEADS), :]
            v_src = vfar_ref.at[layer, g, pl.ds(start, bk * N_HEADS), :]
        k_copy = pltpu.make_async_copy(k_src, kbuf_ref, sem_ref.at[0])
        v_copy = pltpu.make_async_copy(v_src, vbuf_ref, sem_ref.at[1])
        k_copy.start()
        v_copy.start()
        k_copy.wait()
        v_copy.wait()
        carries, accs = split(_round_robin(
            [block(h, kbuf_ref, vbuf_ref, head_rows(bk), lambda h=h: (carries[h], accs[h]), False)
             for h in range(N_HEADS)]))
        return c + 1, carries, accs

    _, _, accs = lax.while_loop(cond, body, (jnp.int32(1), carries, accs))
    for h in range(N_HEADS):
        o_ref[:, _hcols(h)] = accs[h].astype(BF16)


def stick_breaking(zq, k_new, v_new, k_past, v_past, layer, *, batch, seq):
    t = zq.shape[0]
    fresh = k_past is None
    if fresh:
        assert batch == 1
        bq = bk = min(ROW_GROUP, t)
        past = None
        near = pl.BlockSpec((None, bk * N_HEADS, HEAD_DIM), lambda g: (layer, jnp.maximum(g - 1, 0), 0))
        k_far, v_far = k_new, v_new
    else:
        bq = seq
        past = k_past.shape[2] // N_HEADS
        bk = min(ROW_GROUP, past)
        assert past % bk == 0
        last = past // bk - 1
        near = pl.BlockSpec((None, None, bk * N_HEADS, HEAD_DIM), lambda g: (layer, g, last, 0))
        k_far, v_far = k_past, v_past
    blk = (bq, GROUP_W)
    diag = pl.BlockSpec((None, bq * N_HEADS, HEAD_DIM), lambda g: (layer, g, 0))
    any_spec = pl.BlockSpec(memory_space=pl.ANY)
    buf = pltpu.VMEM((bk * N_HEADS, HEAD_DIM), F32)
    return pl.pallas_call(
        functools.partial(_sb_kernel, bq=bq, bk=bk, fresh=fresh, layer=layer, past=past),
        grid=(t // bq,),
        in_specs=[pl.BlockSpec(blk, lambda g: (g, 0)), diag, diag, near, near, any_spec, any_spec],
        out_specs=pl.BlockSpec(blk, lambda g: (g, 0)),
        out_shape=jax.ShapeDtypeStruct((t, GROUP_W), BF16),
        scratch_shapes=[buf, buf, pltpu.SemaphoreType.DMA((2,))],
        compiler_params=_cparams("arbitrary"),
        name="stick_breaking",
    )(zq, k_new, v_new, k_far, v_far, k_far, v_far)


def _out_proj_kernel(x_ref, m0_ref, m1_ref, m2_ref, m3_ref, w_ref, o_ref):
    acc = x_ref[...]
    for gi, m_ref in enumerate((m0_ref, m1_ref, m2_ref, m3_ref)):
        acc = acc + _dot(m_ref[...], w_ref[gi * GROUP_W:(gi + 1) * GROUP_W, :])
    o_ref[...] = acc


def out_proj(x, mixes, w, *, tm=512):
    t, d = x.shape
    tm = min(tm, t)
    mix_spec = pl.BlockSpec((tm, GROUP_W), lambda i: (i, 0))
    return pl.pallas_call(
        _out_proj_kernel,
        grid=(t // tm,),
        in_specs=[pl.BlockSpec((tm, d), lambda i: (i, 0)), mix_spec, mix_spec, mix_spec, mix_spec,
                  pl.BlockSpec(w.shape, lambda i: (0, 0))],
        out_specs=pl.BlockSpec((tm, d), lambda i: (i, 0)),
        out_shape=jax.ShapeDtypeStruct((t, d), F32),
        compiler_params=_cparams("parallel"),
        name="out_proj",
    )(x, *mixes, w)


def _mlp_kernel(x_ref, g_ref, wu_ref, wd_ref, *rest, final, n_cast):
    rest = list(rest)
    fg_ref = rest.pop(0) if final else None
    cast_in = [rest.pop(0) for _ in range(n_cast)]
    o_ref = rest.pop(0)
    cast_out = [rest.pop(0) for _ in range(n_cast)]
    (h_ref,) = rest
    f = pl.program_id(1)

    @pl.when(f == 0)
    def _():
        x = x_ref[...]
        h_ref[...] = _rms(x, g_ref[...]).astype(BF16)
        o_ref[...] = x

    u = _dot(h_ref[...], wu_ref[...])
    a = jnp.square(jnp.maximum(u, 0.0)).astype(BF16)
    o_ref[...] += _dot(a, wd_ref[...])

    for src_ref, dst_ref in zip(cast_in, cast_out):
        dst_ref[...] = src_ref[...].astype(BF16)

    if final:
        @pl.when(f == pl.num_programs(1) - 1)
        def _():
            o_ref[...] = _rms(o_ref[...], fg_ref[...])


def mlp(x, g, w_up, w_down, final_g=None, cast_next=None, *, tm=1024, tf=512):
    t, d = x.shape
    dff = w_up.shape[1]
    tm = min(tm, t)
    nf = dff // tf
    steps = (t // tm) * nf
    final = final_g is not None
    vec_spec = pl.BlockSpec((1, d), lambda i, f: (0, 0))
    in_specs = [pl.BlockSpec((tm, d), lambda i, f: (i, 0)), vec_spec,
                pl.BlockSpec((d, tf), lambda i, f: (0, f)), pl.BlockSpec((tf, d), lambda i, f: (f, 0))]
    args = [x, g.reshape(1, d), w_up, w_down]
    if final:
        in_specs.append(vec_spec)
        args.append(final_g.reshape(1, d))
    out_specs = [pl.BlockSpec((tm, d), lambda i, f: (i, 0))]
    out_shape = [jax.ShapeDtypeStruct((t, d), F32)]
    n_cast = 0
    if cast_next is not None:
        layer, stacked = cast_next
        n_cast = len(stacked)
        for w in stacked:
            _, r, c = w.shape
            slab = r // steps
            assert slab * steps == r and slab % BF16_SUBLANES == 0, (w.shape, steps)
            in_specs.append(pl.BlockSpec((None, slab, c), lambda i, f: (layer, i * nf + f, 0)))
            args.append(w)
            out_specs.append(pl.BlockSpec((slab, c), lambda i, f: (i * nf + f, 0)))
            out_shape.append(jax.ShapeDtypeStruct((r, c), BF16))
    out = pl.pallas_call(
        functools.partial(_mlp_kernel, final=final, n_cast=n_cast),
        grid=(t // tm, nf),
        in_specs=in_specs,
        out_specs=out_specs,
        out_shape=out_shape,
        scratch_shapes=[pltpu.VMEM((tm, d), BF16)],
        compiler_params=_cparams("parallel", "arbitrary"),
        name="mlp",
    )(*args)
    return (out[0], out[1:]) if n_cast else out[0]


def _rope_tables(pos):
    half = HEAD_DIM // 2
    inv = ROPE_BASE ** (-jnp.arange(half, dtype=F32) / half)
    ang = pos.astype(F32)[:, None] * inv[None, :]
    cos = jnp.cos(ang)
    sin = jnp.sin(ang)
    return jnp.concatenate([cos, cos], axis=-1), jnp.concatenate([-sin, sin], axis=-1)


class _Group:
    def __init__(self, x, pos0, states, depth):
        self.batch, self.seq, self.d = x.shape
        self.pos0 = pos0
        t = self.batch * self.seq
        self.x2 = x.reshape(t, self.d)
        rows = min(ROW_GROUP, t)
        self.cos, self.sin = _rope_tables(pos0 + jnp.arange(self.seq, dtype=jnp.int32))
        if states is None:
            self.mix_states = self.k_past = self.v_past = None
        else:
            reps = rows // self.seq
            self.cos, self.sin = jnp.tile(self.cos, (reps, 1)), jnp.tile(self.sin, (reps, 1))
            past = states[1].shape[2]
            self.k_past = states[1].reshape(depth, self.batch, past * N_HEADS, HEAD_DIM)
            self.v_past = states[2].reshape(depth, self.batch, past * N_HEADS, HEAD_DIM)
            pool_s = jnp.pad(states[3], ((0, 0), (0, 0), (POOL_HEAD - POOL_PAD, 0), (0, 0)))
            self.mix_states = (states[0], pool_s, states[4])
        self.depth = depth
        self.kv_stack = None
        self.rets, self.pools, self.hgs = [], [], []

    def layer(self, l, p, lb, wb, final_g, cast_next):
        bs = dict(batch=self.batch, seq=self.seq)
        zq, ks, vs, ro, po, ho, ret_new, hg_new, pool_rows = in_proj_mix(
            self.x2, p["norm1_g"][l], wb[0], self.cos, self.sin, p, lb, l, self.mix_states, self.kv_stack,
            depth=self.depth, pos0=self.pos0, **bs)
        self.kv_stack = (ks, vs)
        so = stick_breaking(zq, ks, vs, self.k_past, self.v_past, l, **bs)
        x1 = out_proj(self.x2, (ro, so, po, ho), wb[3])
        out = mlp(x1, p["norm2_g"][l], wb[1], wb[2], final_g, cast_next)
        self.x2, casts = out if cast_next is not None else (out, None)
        state_shape = (self.batch, N_HEADS, HEAD_DIM, HEAD_DIM)
        self.rets.append(ret_new.reshape(state_shape))
        self.hgs.append(hg_new.reshape(state_shape))
        self.pools.append(pool_rows.reshape(self.batch, POOL_HEAD, GROUP_W)[:, POOL_HEAD - POOL_PAD:, :])
        return casts

    def outputs(self):
        ks, vs = self.kv_stack
        kv_shape = (self.depth, self.batch, self.seq, N_HEADS, HEAD_DIM)
        y = self.x2.reshape(self.batch, self.seq, self.d)
        return (y, jnp.stack(self.rets), ks.reshape(kv_shape), vs.reshape(kv_shape),
                jnp.stack(self.pools), jnp.stack(self.hgs))


def kernel(x_prompt, x_sample, state_ret, cache_sb_k, cache_sb_v, state_pool, state_hgrn, norm1_g, w_in, ret_norm_g, pool_w, pool_scale, hg_lower_bounds, hg_norm_g, w_out, norm2_g, w_up, w_down, final_norm_g):
    lb_all = jnp.cumsum(jax.nn.softmax(hg_lower_bounds.astype(F32), axis=0), axis=0)
    lb_all = lb_all - lb_all[0:1]
    depth = w_in.shape[0]
    params = dict(norm1_g=norm1_g, ret_norm_g=ret_norm_g, pool_w=pool_w.astype(BF16), pool_scale=pool_scale,
                  hg_norm_g=hg_norm_g, norm2_g=norm2_g)
    big = [w_in, w_up, w_down, w_out]
    prompt = _Group(x_prompt, 0, None, depth)
    streams = _Group(x_sample, cache_sb_k.shape[2],
                     (state_ret, cache_sb_k, cache_sb_v, state_pool, state_hgrn), depth)
    wb = tuple(w[0].astype(BF16) for w in big)
    for l in range(depth):
        final_g = final_norm_g if l == depth - 1 else None
        cast_next = (l + 1, big) if l + 1 < depth else None
        wb_next = prompt.layer(l, params, lb_all[l], wb, final_g, cast_next)
        streams.layer(l, params, lb_all[l], wb, final_g, None)
        wb = wb_next
    y_p, ret_p, sbk_p, sbv_p, pool_p, hg_p = prompt.outputs()
    y_s, ret_s, sbk_s, sbv_s, pool_s, hg_s = streams.outputs()
    return (y_p, y_s, ret_p, ret_s, sbk_p, sbv_p, sbk_s, sbv_s, pool_p, pool_s, hg_p, hg_s)
```

```python
import functools
import math

import numpy as np
import jax
import jax.numpy as jnp
from jax import lax
from jax.experimental import pallas as pl
from jax.experimental.pallas import tpu as pltpu

F32 = jnp.float32
BF16 = jnp.bfloat16

HEAD_DIM = 128
BF16_SUBLANES = 16
N_HEADS = 4
GROUP_W = N_HEADS * HEAD_DIM
N_SLOTS = 12
SB_SLOT0, N_SB_SLOTS = 4, 3
POOL_WINDOWS = (2, 4, 8, 16)
POOL_PAD = max(POOL_WINDOWS) - 1
POOL_HEAD = 16
ROPE_BASE = 10000.0
EPS = 1e-6
SB_SCALE = HEAD_DIM ** -0.5
SB_DEAD_LOG = -104.0

VMEM_LIMIT = 60 * 1024 * 1024
ROW_GROUP = 256
HGRN_CHUNK = 64
HGRN_SUB = 128


def _cparams(*sem):
    return pltpu.CompilerParams(dimension_semantics=sem, vmem_limit_bytes=VMEM_LIMIT)


def _dot(a, b):
    return jnp.dot(a, b, preferred_element_type=F32)


def _dot_nt(a, b):
    return lax.dot_general(a, b, (((1,), (1,)), ((), ())), preferred_element_type=F32)


def _sigmoid(x):
    return 1.0 / (1.0 + jnp.exp(-x))


def _log_sigmoid(x):
    return jnp.minimum(x, 0.0) - jnp.log(1.0 + jnp.exp(-jnp.abs(x)))


def _rms(x, g):
    ms = jnp.mean(x * x, axis=-1, keepdims=True)
    return x * lax.rsqrt(ms + EPS) * g


def _hcols(h):
    return slice(h * HEAD_DIM, (h + 1) * HEAD_DIM)


def _slot_cols(slot, h):
    return slice(slot * GROUP_W + h * HEAD_DIM, slot * GROUP_W + (h + 1) * HEAD_DIM)


def _head_norm_gate(o, gn, gate):
    return (_rms(o, gn) * gate).astype(BF16)


def _retention_tables(lg, rows, chunk):
    row = lax.broadcasted_iota(jnp.int32, (rows, rows), 0)
    col = lax.broadcasted_iota(jnp.int32, (rows, rows), 1)
    diff = row - col
    ok = jnp.where((row ^ col) < chunk, diff, -1) >= 0
    dmask = jnp.where(ok, jnp.exp(jnp.maximum(diff, 0).astype(F32) * lg), 0.0)
    pos = (lax.broadcasted_iota(jnp.int32, (rows, HEAD_DIM), 0) & (chunk - 1)).astype(F32)
    return dmask, jnp.exp((pos + 1.0) * lg), jnp.exp((chunk - 1.0 - pos) * lg)


def _retention_head(q, k, v, gate, cos, sin, gn, dmask, qdec, kdec, state_in, chunk, nseq):
    def rope(x):
        return x * cos + pltpu.roll(x, HEAD_DIM // 2, 1) * sin

    q = rope(q)
    k = rope(k) * SB_SCALE
    vb = v.astype(BF16)
    scores = _dot_nt(q.astype(BF16), k.astype(BF16)) * dmask
    o_intra = _dot(scores.astype(BF16), vb)
    qd = (q * qdec).astype(BF16)
    kd = k * kdec
    cdec = qdec[chunk - 1:chunk, :]
    outs, states = [], []
    for b in range(nseq):
        sl = slice(b * chunk, (b + 1) * chunk)
        s_old = state_in(b)
        outs.append(o_intra[sl] + _dot(qd[sl], s_old.astype(BF16)))
        states.append(cdec * s_old + _dot(kd[sl].T.astype(BF16), vb[sl]))
    o = outs[0] if nseq == 1 else jnp.concatenate(outs, axis=0)
    return _head_norm_gate(o, gn, gate * _sigmoid(gate)), states


def _hgrn_tables(rows, chunk):
    nlev = int(math.log2(chunk))
    l = np.arange(rows)[:, None]
    m = np.arange(rows)[None, :]
    same = (l // chunk) == (m // chunk)
    x = np.bitwise_xor(l, m)
    hb = np.floor(np.log2(np.maximum(x, 1))).astype(np.int64)
    level = nlev - 1 - hb
    lv = np.where(same & (l > m), level, np.where(l == m, nlev, -1)).astype(np.float32)
    tri = (same & (l >= m)).astype(np.float32)
    return jnp.asarray(lv, BF16), jnp.asarray(np.tile(tri, (1, 3)), BF16)


def _split3(x):
    h1 = x.astype(BF16)
    r1 = x - h1.astype(F32)
    h2 = r1.astype(BF16)
    h3 = (r1 - h2.astype(F32)).astype(BF16)
    return h1, h2, h3


def _round_robin(gens):
    results = [None] * len(gens)
    live = list(enumerate(gens))
    while live:
        still = []
        for i, gen in live:
            try:
                next(gen)
                still.append((i, gen))
            except StopIteration as done:
                results[i] = done.value
        live = still
    return results


def _hgrn_head(gq, fl, vi, gate, lb, gn, lv, tri, gcum_ref, state_in, chunk, nseq, chained):
    rows = chunk * nseq
    nlev = int(math.log2(chunk))
    q = gq * _sigmoid(gq)
    log_lb, log1m_lb, one_m_lb = lb[0:1, :], lb[1:2, :], lb[2:3, :]
    t = jnp.exp(-jnp.abs(fl))
    r = 1.0 / (1.0 + t)
    sig_neg = jnp.where(fl >= 0.0, t * r, r)
    b = log1m_lb + (jnp.minimum(fl, 0.0) - jnp.log(1.0 + t))
    log_f = jnp.maximum(log_lb, b) + jnp.log(1.0 + jnp.exp(-jnp.abs(log_lb - b)))
    kk = one_m_lb * sig_neg
    vb = vi.astype(BF16)
    yield

    sub = lv.shape[0]
    subs = [slice(r0, r0 + sub) for r0 in range(0, rows, sub)]

    h1, h2, h3 = _split3(log_f)
    gcum = jnp.concatenate([_dot(tri, jnp.concatenate([h1[r], h2[r], h3[r]], axis=0)) for r in subs], axis=0)
    gcum_ref[...] = gcum
    yield

    ridx = lax.broadcasted_iota(jnp.int32, (rows, HEAD_DIM), 0)
    qb = q.astype(BF16)
    kb = kk.astype(BF16)
    zero_scores = jnp.zeros((sub, sub), BF16)
    scores = [jnp.where(lv == nlev, _dot_nt(qb[r], kb[r]).astype(BF16), zero_scores) for r in subs]

    def add_level(scores, level, e):
        w = jnp.exp(e).astype(BF16)
        qw = qb * w
        kw = kb * w
        return [jnp.where(lv == level, _dot_nt(qw[r], kw[r]).astype(BF16), sc) for r, sc in zip(subs, scores)]

    for level in range(nlev):
        blk = chunk >> level
        if blk < 16:
            break
        half = blk // 2
        pieces = [jnp.broadcast_to(gcum_ref[r0 + half - 1:r0 + half, :], (blk, HEAD_DIM))
                  for r0 in range(0, rows, blk)]
        gmid = pieces[0] if len(pieces) == 1 else jnp.concatenate(pieces, axis=0)
        scores = add_level(scores, level, -jnp.abs(gcum - gmid))
        yield

    f1 = pltpu.roll(log_f, 1, 0)
    f2 = pltpu.roll(log_f, 2, 0)
    f3 = pltpu.roll(log_f, 3, 0)
    b1 = pltpu.roll(log_f, rows - 1, 0)
    b2 = pltpu.roll(log_f, rows - 2, 0)
    b3 = pltpu.roll(log_f, rows - 3, 0)
    a1 = log_f
    a2 = a1 + f1
    a3 = a2 + f2
    a4 = a3 + f3
    c1 = b1
    c2 = c1 + b2
    c3 = c2 + b3
    zero = jnp.zeros_like(log_f)

    def pick(idx, table):
        out = table[-1]
        for j in range(len(table) - 2, -1, -1):
            out = jnp.where(idx == j, table[j], out)
        return out

    small = {8: pick(ridx & 7, [c3, c2, c1, zero, a1, a2, a3, a4]),
             4: pick(ridx & 3, [c1, zero, a1, a2]),
             2: pick(ridx & 1, [zero, a1])}
    for blk in (8, 4, 2):
        if blk <= chunk:
            scores = add_level(scores, nlev - int(math.log2(blk)), small[blk])
            yield

    o_intra = jnp.concatenate([_dot(sc, vb[r]) for r, sc in zip(subs, scores)], axis=0)
    qg = (q * jnp.exp(gcum)).astype(BF16)
    yield
    outs, states = [], []
    st_chain = state_in(0) if chained else None
    for bi in range(nseq):
        sl = slice(bi * chunk, (bi + 1) * chunk)
        glast = gcum_ref[(bi + 1) * chunk - 1:(bi + 1) * chunk, :]
        st_old = st_chain if chained else state_in(bi)
        outs.append(o_intra[sl] + _dot_nt(qg[sl], st_old.astype(BF16)))
        kdec = (kk[sl] * jnp.exp(glast - gcum[sl])).astype(BF16)
        st_new = st_old * jnp.exp(glast) + _dot(vi[sl].T.astype(BF16), kdec)
        if chained:
            st_chain = st_new
        else:
            states.append(st_new)
        yield
    if chained:
        states = [st_chain]
    o = outs[0] if nseq == 1 else jnp.concatenate(outs, axis=0)
    return _head_norm_gate(o, gn, _sigmoid(gate)), states


def _pool_group(ext_ref, pw_ref, psc_ref, o_ref, pos1, seq, nseq):
    for gi, w in enumerate(POOL_WINDOWS):
        cols = _hcols(gi)
        cnt = jnp.minimum(pos1, w).astype(F32)
        ps = []
        for b in range(nseq):
            tok = ext_ref[b, POOL_HEAD:POOL_HEAD + seq, cols]
            wsum = tok
            for i in range(1, w):
                wsum = wsum + ext_ref[b, POOL_HEAD - i:POOL_HEAD - i + seq, cols]
            ps.append(wsum / cnt - tok)
        p = ps[0] if nseq == 1 else jnp.concatenate(ps, axis=0)
        y = _dot(p.astype(BF16), pw_ref[gi]) * psc_ref[:, cols]
        o_ref[:, cols] = y.astype(BF16)


_KEEP_RET, _KEEP_POOL, _KEEP_HGRN = 0, 4, 5
_N_KEEP = 9


def _in_mix_kernel(lg_ref, x_ref, g1_ref, w_ref, cos_ref, sin_ref, rgn_ref, pw_ref, psc_ref, lb_ref, hgn_ref,
                   lv_ref, tri_ref, *rest, seq, nseq, carry, pos0, n_groups, kv_alias, layer):
    rest = list(rest)
    if carry:
        rs0_ref = ps0_ref = hs0_ref = None
    else:
        rs0_ref, ps0_ref, hs0_ref = rest[:3]
        del rest[:3]
    if kv_alias:
        del rest[:2]
    zq_ref, ko_ref, vo_ref, ro_ref, po_ref, ho_ref, rs_out, hs_out, pc_out = rest[:9]
    zr, dmask_ref, qdec_ref, kdec_ref, gcum_ref, ext_ref = rest[9:15]
    rstate_ref, hstate_ref = rest[15:] if carry else (None, None)
    rows = seq * nseq
    ret_chunk = seq
    hg_chunk = min(HGRN_CHUNK, seq)
    g = pl.program_id(0)

    @pl.when(g == 0)
    def _():
        for h in range(N_HEADS):
            dmask_ref[h], qdec_ref[h], kdec_ref[h] = _retention_tables(lg_ref[h], rows, ret_chunk)
        if carry:
            rstate_ref[...] = jnp.zeros_like(rstate_ref)
            hstate_ref[...] = jnp.zeros_like(hstate_ref)
            ext_ref[:, 0:POOL_HEAD, :] = jnp.zeros((nseq, POOL_HEAD, GROUP_W), F32)

    hx = _rms(x_ref[...], g1_ref[...]).astype(BF16)
    slots = iter(range(N_SLOTS))

    def project(n):
        for _ in range(n):
            s = next(slots)
            zs = _dot(hx, w_ref[:, s * GROUP_W:(s + 1) * GROUP_W])
            if s == SB_SLOT0:
                zq_ref[...] = zs
            elif SB_SLOT0 < s < SB_SLOT0 + N_SB_SLOTS:
                kv_ref = ko_ref if s == SB_SLOT0 + 1 else vo_ref
                if not kv_alias:
                    for other in range(kv_ref.shape[0]):
                        if other != layer:
                            kv_ref[other] = jnp.zeros(kv_ref.shape[1:], F32)
                    kv_ref = kv_ref.at[layer]
                for h in range(N_HEADS):
                    kv_ref[pl.ds(h, rows, stride=N_HEADS), :] = zs[:, _hcols(h)]
            else:
                k = s if s < SB_SLOT0 else s - N_SB_SLOTS
                zr[:, k * GROUP_W:(k + 1) * GROUP_W] = zs

    project(4)
    cos = cos_ref[...]
    sin = sin_ref[...]
    for h in range(N_HEADS):
        project(1)
        state_in = (lambda b, h=h: rstate_ref[h]) if carry else (lambda b, h=h: rs0_ref[b, h])
        o, states = _retention_head(
            zr[:, _slot_cols(_KEEP_RET + 0, h)], zr[:, _slot_cols(_KEEP_RET + 1, h)],
            zr[:, _slot_cols(_KEEP_RET + 2, h)], zr[:, _slot_cols(_KEEP_RET + 3, h)],
            cos, sin, rgn_ref[:, _hcols(h)], dmask_ref[h], qdec_ref[h], kdec_ref[h], state_in, ret_chunk, nseq)
        ro_ref[:, _hcols(h)] = o
        for b, st in enumerate(states):
            if carry:
                rstate_ref[h] = st
            else:
                rs_out[b, h] = st

    project(2)
    if carry:
        pos_base = g * rows + pos0
    else:
        pos_base = pos0
    for b in range(nseq):
        if not carry:
            ext_ref[b, 0:POOL_HEAD, :] = ps0_ref[b]
        ext_ref[b, POOL_HEAD:POOL_HEAD + seq, :] = zr[b * seq:(b + 1) * seq,
                                                      _KEEP_POOL * GROUP_W:(_KEEP_POOL + 1) * GROUP_W]
    pos1 = lax.broadcasted_iota(jnp.int32, (seq, HEAD_DIM), 0) + (pos_base + 1)
    _pool_group(ext_ref, pw_ref, psc_ref, po_ref, pos1, seq, nseq)
    for b in range(nseq):
        tail = ext_ref[b, seq:seq + POOL_HEAD, :]
        if carry:
            ext_ref[b, 0:POOL_HEAD, :] = tail
            pc_out[...] = tail
        else:
            pc_out[b] = tail

    project(2)
    lv = lv_ref[...]
    tri = tri_ref[...]
    hg_nseq = rows // hg_chunk
    heads = []
    for h in range(N_HEADS):
        state_in = (lambda b, h=h: hstate_ref[h]) if carry else (lambda b, h=h: hs0_ref[b, h].T)
        heads.append(_hgrn_head(
            zr[:, _slot_cols(_KEEP_HGRN + 0, h)], zr[:, _slot_cols(_KEEP_HGRN + 1, h)],
            zr[:, _slot_cols(_KEEP_HGRN + 2, h)], zr[:, _slot_cols(_KEEP_HGRN + 3, h)],
            lb_ref[:, _hcols(h)], hgn_ref[:, _hcols(h)], lv, tri, gcum_ref.at[h], state_in,
            hg_chunk, hg_nseq, chained=carry))
    for h, (o, states) in enumerate(_round_robin(heads)):
        ho_ref[:, _hcols(h)] = o
        for b, st in enumerate(states):
            if carry:
                hstate_ref[h] = st
            else:
                hs_out[b, h] = st.T

    assert next(slots, None) is None

    if carry:
        @pl.when(g == n_groups - 1)
        def _():
            rs_out[...] = rstate_ref[...]
            for h in range(N_HEADS):
                hs_out[h] = hstate_ref[h].T


def in_proj_mix(x, g1, w, cos, sin, p, lb, l, states, kv_stack, *, depth, batch, seq, pos0):
    t, d = x.shape
    carry = states is None
    rows = min(ROW_GROUP, t)
    if carry:
        assert batch == 1 and t % rows == 0
        kseq, nseq = rows, 1
    else:
        assert rows % seq == 0 and batch % (rows // seq) == 0
        kseq, nseq = seq, rows // seq
    n_groups = t // rows
    hg_chunk = min(HGRN_CHUNK, kseq)
    sub = min(HGRN_SUB, rows)
    assert sub % hg_chunk == 0 and rows % sub == 0
    lv, tri = _hgrn_tables(sub, hg_chunk)
    lg = jnp.asarray(np.log(1.0 - 2.0 ** (-5.0 - np.arange(N_HEADS))), F32)
    lb_tab = jnp.stack([jnp.log(lb), jnp.log1p(-lb), 1.0 - lb], axis=0)

    grp = lambda g: (g, 0)
    const = lambda g: (0, 0)
    tab_idx = grp if carry else const
    vec = pl.BlockSpec((1, GROUP_W), const)
    in_specs = [
        pl.BlockSpec(memory_space=pltpu.SMEM),
        pl.BlockSpec((rows, d), grp), pl.BlockSpec((1, d), const), pl.BlockSpec(w.shape, const),
        pl.BlockSpec((rows, HEAD_DIM), tab_idx), pl.BlockSpec((rows, HEAD_DIM), tab_idx), vec,
        pl.BlockSpec((None,) + p["pool_w"].shape[1:], lambda g: (l, 0, 0, 0)), vec,
        pl.BlockSpec((3, GROUP_W), const), vec,
        pl.BlockSpec((sub, sub), const), pl.BlockSpec((sub, 3 * sub), const),
    ]
    args = [lg, x, g1.reshape(1, d), w, cos, sin, p["ret_norm_g"][l].reshape(1, GROUP_W),
            p["pool_w"], p["pool_scale"][l].reshape(1, GROUP_W), lb_tab, p["hg_norm_g"][l].reshape(1, GROUP_W),
            lv, tri]
    mix_spec = pl.BlockSpec((rows, GROUP_W), grp)
    mix_shape = jax.ShapeDtypeStruct((t, GROUP_W), BF16)
    state_blk = (nseq, N_HEADS, HEAD_DIM, HEAD_DIM)
    if carry:
        whole = lambda shape: pl.BlockSpec(shape, lambda g: (0,) * len(shape))
        st_shape = (N_HEADS, HEAD_DIM, HEAD_DIM)
        state_specs = [whole(st_shape), whole(st_shape), whole((POOL_HEAD, GROUP_W))]
        state_shapes = [jax.ShapeDtypeStruct(st_shape, F32), jax.ShapeDtypeStruct(st_shape, F32),
                        jax.ShapeDtypeStruct((POOL_HEAD, GROUP_W), F32)]
    else:
        ret_s, pool_s, hg_s = states
        st_in = pl.BlockSpec((None,) + state_blk, lambda g: (l, g, 0, 0, 0))
        pool_in = pl.BlockSpec((None, nseq, POOL_HEAD, GROUP_W), lambda g: (l, g, 0, 0))
        in_specs += [st_in, pool_in, st_in]
        args += [ret_s, pool_s, hg_s]
        st_out = pl.BlockSpec(state_blk, lambda g: (g, 0, 0, 0))
        state_specs = [st_out, st_out, pl.BlockSpec((nseq, POOL_HEAD, GROUP_W), lambda g: (g, 0, 0))]
        state_shapes = [jax.ShapeDtypeStruct(ret_s.shape[1:], F32), jax.ShapeDtypeStruct(hg_s.shape[1:], F32),
                        jax.ShapeDtypeStruct((batch, POOL_HEAD, GROUP_W), F32)]
    scratch = [
        pltpu.VMEM((rows, _N_KEEP * GROUP_W), F32),
        pltpu.VMEM((N_HEADS, rows, rows), F32), pltpu.VMEM((N_HEADS, rows, HEAD_DIM), F32),
        pltpu.VMEM((N_HEADS, rows, HEAD_DIM), F32), pltpu.VMEM((N_HEADS, rows, HEAD_DIM), F32),
        pltpu.VMEM((nseq, POOL_HEAD + kseq, GROUP_W), F32),
    ]
    if carry:
        scratch += [pltpu.VMEM((N_HEADS, HEAD_DIM, HEAD_DIM), F32), pltpu.VMEM((N_HEADS, HEAD_DIM, HEAD_DIM), F32)]
    kv_alias = kv_stack is not None
    aliases = {}
    if kv_alias:
        aliases = {len(args): 1, len(args) + 1: 2}
        in_specs += [pl.BlockSpec(memory_space=pl.ANY)] * 2
        args += list(kv_stack)
    kv_shape = jax.ShapeDtypeStruct((depth, t * N_HEADS, HEAD_DIM), F32)
    if kv_alias:
        kv_spec = pl.BlockSpec((None, rows * N_HEADS, HEAD_DIM), lambda g: (l, g, 0))
    else:
        kv_spec = pl.BlockSpec((depth, rows * N_HEADS, HEAD_DIM), lambda g: (0, g, 0))
    return pl.pallas_call(
        functools.partial(_in_mix_kernel, seq=kseq, nseq=nseq, carry=carry, pos0=pos0, n_groups=n_groups,
                          kv_alias=kv_alias, layer=l),
        grid=(n_groups,),
        in_specs=in_specs,
        out_specs=[pl.BlockSpec((rows, GROUP_W), grp), kv_spec, kv_spec, mix_spec, mix_spec, mix_spec,
                   *state_specs],
        out_shape=[jax.ShapeDtypeStruct((t, GROUP_W), F32), kv_shape, kv_shape, mix_shape, mix_shape, mix_shape,
                   *state_shapes],
        scratch_shapes=scratch,
        input_output_aliases=aliases,
        compiler_params=_cparams("arbitrary"),
        name="in_proj_mix",
    )(*args)


def _strict_upper_ones_twice(n):
    row = lax.broadcasted_iota(jnp.int32, (2 * n, n), 0) & (n - 1)
    col = lax.broadcasted_iota(jnp.int32, (2 * n, n), 1)
    return jnp.where(row > col, 1.0, 0.0).astype(BF16)


def _sb_block(qb, kblk, vblk, state, *, diag):
    bq, bk = qb.shape[0], kblk.shape[0]
    nz = _dot_nt(qb, kblk.astype(BF16)) * (-SB_SCALE)
    yield
    lf = _log_sigmoid(nz)
    if diag:
        row = lax.broadcasted_iota(jnp.int32, (bq, bk), 0)
        col = lax.broadcasted_iota(jnp.int32, (bq, bk), 1)
        valid = col < row
        lf = jnp.where(valid, lf, 0.0)
    hi = lf.astype(BF16)
    lo = (lf - hi.astype(F32)).astype(BF16)
    later_in = _dot(jnp.concatenate([hi, lo], axis=1), _strict_upper_ones_twice(bk))
    yield
    carry, acc = state()
    a = jnp.exp(lf - nz + later_in + carry)
    if diag:
        a = jnp.where(valid, a, 0.0)
    acc = acc + _dot(a.astype(BF16), vblk.astype(BF16))
    carry = carry + later_in[:, 0:1] + lf[:, 0:1]
    return carry, acc


def _sb_kernel(q_ref, kd_ref, vd_ref, kp_ref, vp_ref, kfar_ref, vfar_ref, o_ref, kbuf_ref, vbuf_ref, sem_ref,
               *, bq, bk, fresh, layer, past):
    g = pl.program_id(0)
    qs = [q_ref[:, _hcols(h)].astype(BF16) for h in range(N_HEADS)]

    def head_rows(n):
        return lambda ref, h: ref[pl.ds(h, n, stride=N_HEADS), :]

    n_blocks = g if fresh else past // bk

    def block(h, k_ref, v_ref, get, state, diag):
        return _sb_block(qs[h], get(k_ref, h), get(v_ref, h), state, diag=diag)

    def keep(gen, holder, h):
        holder[h] = yield from gen

    def split(results):
        return tuple(c for c, _ in results), tuple(a for _, a in results)

    def alive(carries):
        m = functools.reduce(jnp.maximum, carries)
        return jnp.max(m) > SB_DEAD_LOG

    zero_state = (jnp.zeros((bq, 1), F32), jnp.zeros((bq, HEAD_DIM), F32))
    diag_res, near_res = [None] * N_HEADS, [None] * N_HEADS
    _round_robin(
        [keep(block(h, kd_ref, vd_ref, head_rows(bq), lambda: zero_state, True), diag_res, h)
         for h in range(N_HEADS)] +
        [keep(block(h, kp_ref, vp_ref, head_rows(bk), lambda h=h: diag_res[h], False), near_res, h)
         for h in range(N_HEADS)])
    carries, accs = split(near_res)
    if fresh:
        has_past = g >= 1
        carries, accs = jax.tree.map(lambda new, old: jnp.where(has_past, new, old),
                                     (carries, accs), split(diag_res))

    def cond(st):
        c, carries, _ = st
        return jnp.logical_and(c < n_blocks, alive(carries))

    def body(st):
        c, carries, accs = st
        if fresh:
            start = pl.multiple_of((g - 1 - c) * bk * N_HEADS, bk * N_HEADS)
            k_src = kfar_ref.at[layer, pl.ds(start, bk * N_HEADS), :]
            v_src = vfar_ref.at[layer, pl.ds(start, bk * N_HEADS), :]
        else:
            start = pl.multiple_of((past - (c + 1) * bk) * N_HEADS, bk * N_HEADS)
            k_src = kfar_ref.at[layer, g, pl.ds(start, bk * N_HEADS), :]
            v_src = vfar_ref.at[layer, g, pl.ds(start, bk * N_HEADS), :]
        k_copy = pltpu.make_async_copy(k_src, kbuf_ref, sem_ref.at[0])
        v_copy = pltpu.make_async_copy(v_src, vbuf_ref, sem_ref.at[1])
        k_copy.start()
        v_copy.start()
        k_copy.wait()
        v_copy.wait()
        carries, accs = split(_round_robin(
            [block(h, kbuf_ref, vbuf_ref, head_rows(bk), lambda h=h: (carries[h], accs[h]), False)
             for h in range(N_HEADS)]))
        return c + 1, carries, accs

    _, _, accs = lax.while_loop(cond, body, (jnp.int32(1), carries, accs))
    for h in range(N_HEADS):
        o_ref[:, _hcols(h)] = accs[h].astype(BF16)


def stick_breaking(zq, k_new, v_new, k_past, v_past, layer, *, batch, seq):
    t = zq.shape[0]
    fresh = k_past is None
    if fresh:
        assert batch == 1
        bq = bk = min(ROW_GROUP, t)
        past = None
        near = pl.BlockSpec((None, bk * N_HEADS, HEAD_DIM), lambda g: (layer, jnp.maximum(g - 1, 0), 0))
        k_far, v_far = k_new, v_new
    else:
        bq = seq
        past = k_past.shape[2] // N_HEADS
        bk = min(ROW_GROUP, past)
        assert past % bk == 0
        last = past // bk - 1
        near = pl.BlockSpec((None, None, bk * N_HEADS, HEAD_DIM), lambda g: (layer, g, last, 0))
        k_far, v_far = k_past, v_past
    blk = (bq, GROUP_W)
    diag = pl.BlockSpec((None, bq * N_HEADS, HEAD_DIM), lambda g: (layer, g, 0))
    any_spec = pl.BlockSpec(memory_space=pl.ANY)
    buf = pltpu.VMEM((bk * N_HEADS, HEAD_DIM), F32)
    return pl.pallas_call(
        functools.partial(_sb_kernel, bq=bq, bk=bk, fresh=fresh, layer=layer, past=past),
        grid=(t // bq,),
        in_specs=[pl.BlockSpec(blk, lambda g: (g, 0)), diag, diag, near, near, any_spec, any_spec],
        out_specs=pl.BlockSpec(blk, lambda g: (g, 0)),
        out_shape=jax.ShapeDtypeStruct((t, GROUP_W), BF16),
        scratch_shapes=[buf, buf, pltpu.SemaphoreType.DMA((2,))],
        compiler_params=_cparams("arbitrary"),
        name="stick_breaking",
    )(zq, k_new, v_new, k_far, v_far, k_far, v_far)


def _out_proj_kernel(x_ref, m0_ref, m1_ref, m2_ref, m3_ref, w_ref, o_ref):
    acc = x_ref[...]
    for gi, m_ref in enumerate((m0_ref, m1_ref, m2_ref, m3_ref)):
        acc = acc + _dot(m_ref[...], w_ref[gi * GROUP_W:(gi + 1) * GROUP_W, :])
    o_ref[...] = acc


def out_proj(x, mixes, w, *, tm=512):
    t, d = x.shape
    tm = min(tm, t)
    mix_spec = pl.BlockSpec((tm, GROUP_W), lambda i: (i, 0))
    return pl.pallas_call(
        _out_proj_kernel,
        grid=(t // tm,),
        in_specs=[pl.BlockSpec((tm, d), lambda i: (i, 0)), mix_spec, mix_spec, mix_spec, mix_spec,
                  pl.BlockSpec(w.shape, lambda i: (0, 0))],
        out_specs=pl.BlockSpec((tm, d), lambda i: (i, 0)),
        out_shape=jax.ShapeDtypeStruct((t, d), F32),
        compiler_params=_cparams("parallel"),
        name="out_proj",
    )(x, *mixes, w)


def _mlp_kernel(x_ref, g_ref, wu_ref, wd_ref, m0_ref, m1_ref, m2_ref, m3_ref, wo_ref, *rest, final, n_cast):
    rest = list(rest)
    fg_ref = rest.pop(0) if final else None
    cast_in = [rest.pop(0) for _ in range(n_cast)]
    o_ref = rest.pop(0)
    cast_out = [rest.pop(0) for _ in range(n_cast)]
    (h_ref,) = rest
    f = pl.program_id(1)

    @pl.when(f == 0)
    def _():
        x = x_ref[...]
        for gi, m_ref in enumerate((m0_ref, m1_ref, m2_ref, m3_ref)):
            x = x + _dot(m_ref[...], wo_ref[gi * GROUP_W:(gi + 1) * GROUP_W, :])
        h_ref[...] = _rms(x, g_ref[...]).astype(BF16)
        o_ref[...] = x

    u = _dot(h_ref[...], wu_ref[...])
    a = jnp.square(jnp.maximum(u, 0.0)).astype(BF16)
    o_ref[...] += _dot(a, wd_ref[...])

    for src_ref, dst_ref in zip(cast_in, cast_out):
        dst_ref[...] = src_ref[...].astype(BF16)

    if final:
        @pl.when(f == pl.num_programs(1) - 1)
        def _():
            o_ref[...] = _rms(o_ref[...], fg_ref[...])


def mlp(x, mixes, w_out, g, w_up, w_down, final_g=None, cast_next=None, *, tm=512, tf=1024):
    t, d = x.shape
    dff = w_up.shape[1]
    tm = min(tm, t)
    nf = dff // tf
    steps = (t // tm) * nf
    final = final_g is not None
    vec_spec = pl.BlockSpec((1, d), lambda i, f: (0, 0))
    mix_spec = pl.BlockSpec((tm, GROUP_W), lambda i, f: (i, 0))
    in_specs = [pl.BlockSpec((tm, d), lambda i, f: (i, 0)), vec_spec,
                pl.BlockSpec((d, tf), lambda i, f: (0, f)), pl.BlockSpec((tf, d), lambda i, f: (f, 0)),
                mix_spec, mix_spec, mix_spec, mix_spec, pl.BlockSpec(w_out.shape, lambda i, f: (0, 0))]
    args = [x, g.reshape(1, d), w_up, w_down, *mixes, w_out]
    if final:
        in_specs.append(vec_spec)
        args.append(final_g.reshape(1, d))
    out_specs = [pl.BlockSpec((tm, d), lambda i, f: (i, 0))]
    out_shape = [jax.ShapeDtypeStruct((t, d), F32)]
    n_cast = 0
    if cast_next is not None:
        layer, stacked = cast_next
        n_cast = len(stacked)
        for w in stacked:
            _, r, c = w.shape
            slab = r // steps
            assert slab * steps == r and slab % BF16_SUBLANES == 0, (w.shape, steps)
            in_specs.append(pl.BlockSpec((None, slab, c), lambda i, f: (layer, i * nf + f, 0)))
            args.append(w)
            out_specs.append(pl.BlockSpec((slab, c), lambda i, f: (i * nf + f, 0)))
            out_shape.append(jax.ShapeDtypeStruct((r, c), BF16))
    out = pl.pallas_call(
        functools.partial(_mlp_kernel, final=final, n_cast=n_cast),
        grid=(t // tm, nf),
        in_specs=in_specs,
        out_specs=out_specs,
        out_shape=out_shape,
        scratch_shapes=[pltpu.VMEM((tm, d), BF16)],
        compiler_params=_cparams("parallel", "arbitrary"),
        name="mlp",
    )(*args)
    return (out[0], out[1:]) if n_cast else out[0]


def _rope_tables(pos):
    half = HEAD_DIM // 2
    inv = ROPE_BASE ** (-jnp.arange(half, dtype=F32) / half)
    ang = pos.astype(F32)[:, None] * inv[None, :]
    cos = jnp.cos(ang)
    sin = jnp.sin(ang)
    return jnp.concatenate([cos, cos], axis=-1), jnp.concatenate([-sin, sin], axis=-1)


class _Group:
    def __init__(self, x, pos0, states, depth):
        self.batch, self.seq, self.d = x.shape
        self.pos0 = pos0
        t = self.batch * self.seq
        self.x2 = x.reshape(t, self.d)
        rows = min(ROW_GROUP, t)
        self.cos, self.sin = _rope_tables(pos0 + jnp.arange(self.seq, dtype=jnp.int32))
        if states is None:
            self.mix_states = self.k_past = self.v_past = None
        else:
            reps = rows // self.seq
            self.cos, self.sin = jnp.tile(self.cos, (reps, 1)), jnp.tile(self.sin, (reps, 1))
            past = states[1].shape[2]
            self.k_past = states[1].reshape(depth, self.batch, past * N_HEADS, HEAD_DIM)
            self.v_past = states[2].reshape(depth, self.batch, past * N_HEADS, HEAD_DIM)
            pool_s = jnp.pad(states[3], ((0, 0), (0, 0), (POOL_HEAD - POOL_PAD, 0), (0, 0)))
            self.mix_states = (states[0], pool_s, states[4])
        self.depth = depth
        self.kv_stack = None
        self.rets, self.pools, self.hgs = [], [], []

    def layer(self, l, p, lb, wb, final_g, cast_next):
        bs = dict(batch=self.batch, seq=self.seq)
        zq, ks, vs, ro, po, ho, ret_new, hg_new, pool_rows = in_proj_mix(
            self.x2, p["norm1_g"][l], wb[0], self.cos, self.sin, p, lb, l, self.mix_states, self.kv_stack,
            depth=self.depth, pos0=self.pos0, **bs)
        self.kv_stack = (ks, vs)
        so = stick_breaking(zq, ks, vs, self.k_past, self.v_past, l, **bs)
        out = mlp(self.x2, (ro, so, po, ho), wb[3], p["norm2_g"][l], wb[1], wb[2], final_g, cast_next)
        self.x2, casts = out if cast_next is not None else (out, None)
        state_shape = (self.batch, N_HEADS, HEAD_DIM, HEAD_DIM)
        self.rets.append(ret_new.reshape(state_shape))
        self.hgs.append(hg_new.reshape(state_shape))
        self.pools.append(pool_rows.reshape(self.batch, POOL_HEAD, GROUP_W)[:, POOL_HEAD - POOL_PAD:, :])
        return casts

    def outputs(self):
        ks, vs = self.kv_stack
        kv_shape = (self.depth, self.batch, self.seq, N_HEADS, HEAD_DIM)
        y = self.x2.reshape(self.batch, self.seq, self.d)
        return (y, jnp.stack(self.rets), ks.reshape(kv_shape), vs.reshape(kv_shape),
                jnp.stack(self.pools), jnp.stack(self.hgs))


def kernel(x_prompt, x_sample, state_ret, cache_sb_k, cache_sb_v, state_pool, state_hgrn, norm1_g, w_in, ret_norm_g, pool_w, pool_scale, hg_lower_bounds, hg_norm_g, w_out, norm2_g, w_up, w_down, final_norm_g):
    lb_all = jnp.cumsum(jax.nn.softmax(hg_lower_bounds.astype(F32), axis=0), axis=0)
    lb_all = lb_all - lb_all[0:1]
    depth = w_in.shape[0]
    params = dict(norm1_g=norm1_g, ret_norm_g=ret_norm_g, pool_w=pool_w.astype(BF16), pool_scale=pool_scale,
                  hg_norm_g=hg_norm_g, norm2_g=norm2_g)
    big = [w_in, w_up, w_down, w_out]
    prompt = _Group(x_prompt, 0, None, depth)
    streams = _Group(x_sample, cache_sb_k.shape[2],
                     (state_ret, cache_sb_k, cache_sb_v, state_pool, state_hgrn), depth)
    wb = tuple(w[0].astype(BF16) for w in big)
    for l in range(depth):
        final_g = final_norm_g if l == depth - 1 else None
        cast_next = (l + 1, big) if l + 1 < depth else None
        wb_next = prompt.layer(l, params, lb_all[l], wb, final_g, cast_next)
        streams.layer(l, params, lb_all[l], wb, final_g, None)
        wb = wb_next
    y_p, ret_p, sbk_p, sbv_p, pool_p, hg_p = prompt.outputs()
    y_s, ret_s, sbk_s, sbv_s, pool_s, hg_s = streams.outputs()
    return (y_p, y_s, ret_p, ret_s, sbk_p, sbv_p, sbk_s, sbv_s, pool_p, pool_s, hg_p, hg_s)
```
